```python
import jax, jax.numpy as jnp
from jax import lax
import numpy as np

D_MODEL = 2048
BATCH = 4
SEQ = 2048
DEPTH = 4

N_MIXERS = 3
N_ATTN_LAYERS = (DEPTH + 2) // 3
N_RWKV_LAYERS = (DEPTH + 1) // 3
N_CONV_LAYERS = DEPTH // 3
PLE_DIM = 256
NORM_EPS = 1e-6
NEG_INF = -1e30

ATTN_GROUPS = ((128, 1), (512, 4), (2048, 16))
N_GROUPS = 3
ATTN_HEADS = 16
ATTN_HEAD_DIM = D_MODEL // ATTN_HEADS
ATTN_BLOCK = 128

RWKV_HEAD_SIZE = 64
RWKV_HEADS = D_MODEL // RWKV_HEAD_SIZE
RWKV_DECAY_LORA = 96
RWKV_A_LORA = 96
RWKV_GATE_LORA = 256
RWKV_GN_EPS = 6.4e-4

CONV_WIDTH = 3

D_FF = 5632
FFN_CONV_WIDTH = 3

kernel_name = "hybrid_dilatedattn_rwkv7_shortconv_convffn"


def rms_norm(x, g):
    xf = x.astype(jnp.float32)
    y = xf * lax.rsqrt(jnp.mean(xf * xf, axis=-1, keepdims=True) + NORM_EPS)
    return (y * g.astype(jnp.float32)).astype(x.dtype)


def causal_dwconv(u, w):
    width = w.shape[0]
    s = u.shape[1]
    up = jnp.pad(u, ((0, 0), (width - 1, 0), (0, 0)))
    y = up[:, 0:s] * w[0]
    for j in range(1, width):
        y = y + up[:, j:j + s] * w[j]
    return y


def alibi_slopes():
    n = N_GROUPS * ATTN_HEADS
    idx = jnp.arange(1, n + 1, dtype=jnp.float32)
    return (2.0 ** (-8.0 * idx / n)).reshape(N_GROUPS, ATTN_HEADS)


def dilated_window_attention(q, k, v, window, dilation, slopes):
    b, s, h, e = q.shape
    n_back = window // dilation
    L = s // dilation
    nb = -(-L // ATTN_BLOCK)
    Lp = nb * ATTN_BLOCK

    def to_blocks(t):
        t = t.reshape(b, L, dilation, h, e)
        t = jnp.pad(t, ((0, 0), (0, Lp - L), (0, 0), (0, 0), (0, 0)))
        return t.reshape(b, nb, ATTN_BLOCK, dilation, h, e)

    def with_prev(t):
        prev = jnp.pad(t, ((0, 0), (1, 0), (0, 0), (0, 0), (0, 0), (0, 0)))[:, :-1]
        return jnp.concatenate([prev, t], axis=2)

    qb = to_blocks(q)
    kw = with_prev(to_blocks(k))
    vw = with_prev(to_blocks(v))

    scale = ATTN_HEAD_DIM ** -0.5
    scores = jnp.einsum('bnqrhe,bnkrhe->bnrhqk', qb, kw,
                        preferred_element_type=jnp.float32) * scale
    qi = jnp.arange(ATTN_BLOCK)[:, None]
    kj = jnp.arange(2 * ATTN_BLOCK)[None, :]
    dist = qi + ATTN_BLOCK - kj
    first = (jnp.arange(nb) == 0)[:, None, None]
    valid = (dist >= 0) & (dist <= n_back) & jnp.logical_not(first & (kj < ATTN_BLOCK))
    bias = -(slopes.astype(jnp.float32)[:, None, None] * (dist * dilation).astype(jnp.float32)[None])
    scores = scores + bias[None, None, None]
    scores = jnp.where(valid[None, :, None, None], scores, NEG_INF)
    m = jnp.max(scores, axis=-1, keepdims=True)
    pr = jnp.exp(scores - m)
    denom = jnp.sum(pr, axis=-1, keepdims=True)
    out = jnp.einsum('bnrhqk,bnkrhe->bnqrhe', (pr / denom).astype(v.dtype), vw)
    lse = (m + jnp.log(denom))[..., 0]
    out = out.reshape(b, Lp, dilation, h, e)[:, :L].reshape(b, s, h, e)
    lse = lse.transpose(0, 1, 4, 2, 3).reshape(b, Lp, dilation, h)[:, :L].reshape(b, s, h)
    return out, lse


def attention_mixer(h, w_qkv, w_o, slopes):
    b, s, _ = h.shape
    qkv = (h @ w_qkv).reshape(b, s, N_GROUPS, 3, ATTN_HEADS, ATTN_HEAD_DIM)
    outs, lses = [], []
    for g, (window, dil) in enumerate(ATTN_GROUPS):
        o, l = dilated_window_attention(qkv[:, :, g, 0], qkv[:, :, g, 1], qkv[:, :, g, 2],
                                        window, dil, slopes[g])
        outs.append(o)
        lses.append(l)
    outs = jnp.stack(outs, 0)
    alpha = jax.nn.softmax(jnp.stack(lses, 0), axis=0)
    o = jnp.sum(alpha[..., None].astype(outs.dtype) * outs, axis=0)
    return o.reshape(b, s, ATTN_HEADS * ATTN_HEAD_DIM) @ w_o


def rwkv7_mixer(h, mu, w_rkv, w0, w_w1, w_w2, a0, w_a1, w_a2, w_g1, w_g2,
                k_k, k_a, r_k, ln_g, ln_b, w_o):
    b, s, d = h.shape
    H, N = RWKV_HEADS, RWKV_HEAD_SIZE
    xx = jnp.pad(h, ((0, 0), (1, 0), (0, 0)))[:, :-1] - h
    x_rkv = h[None] + xx[None] * mu[:3, None, None, :]
    xw = h + xx * mu[3]
    xa = h + xx * mu[4]
    xg = h + xx * mu[5]
    rkv = jnp.einsum('nbsd,nde->nbse', x_rkv, w_rkv)
    r, k, v = rkv[0], rkv[1], rkv[2]
    w_log = -jax.nn.softplus(-(w0 + jnp.tanh(xw @ w_w1) @ w_w2)) - 0.5
    a = jax.nn.sigmoid(a0 + (xa @ w_a1) @ w_a2)
    g = jax.nn.sigmoid(xg @ w_g1) @ w_g2

    kk = (k * k_k).reshape(b, s, H, N).astype(jnp.float32)
    kk = kk / jnp.maximum(jnp.sqrt(jnp.sum(kk * kk, axis=-1, keepdims=True)), 1e-12)
    k = k * (1 + (a - 1) * k_a)
    decay = jnp.exp(-jnp.exp(w_log.astype(jnp.float32)))

    rh = r.reshape(b, s, H, N)
    kh = k.reshape(b, s, H, N)
    vh = v.reshape(b, s, H, N)
    ah = a.reshape(b, s, H, N).astype(jnp.float32)

    def tm(t):
        return jnp.moveaxis(t.astype(jnp.float32), 1, 0)

    seq_in = (tm(rh), tm(decay.reshape(b, s, H, N)), tm(kh), tm(vh), tm(-kk), tm(kk * ah))

    def step(state, inp):
        r_t, w_t, k_t, v_t, a_t, b_t = inp
        sa = jnp.einsum('bhvk,bhk->bhv', state, a_t)
        state = (state * w_t[:, :, None, :] + sa[..., None] * b_t[:, :, None, :]
                 + v_t[..., None] * k_t[:, :, None, :])
        y_t = jnp.einsum('bhvk,bhk->bhv', state, r_t)
        return state, y_t

    state0 = jnp.zeros((b, H, N, N), jnp.float32)
    _, y = lax.scan(step, state0, seq_in)
    y = jnp.moveaxis(y, 0, 1)
    mean = jnp.mean(y, axis=-1, keepdims=True)
    var = jnp.mean(jnp.square(y - mean), axis=-1, keepdims=True)
    y = ((y - mean) * lax.rsqrt(var + RWKV_GN_EPS)).reshape(b, s, d)
    y = (y * ln_g.astype(jnp.float32) + ln_b.astype(jnp.float32)).astype(h.dtype)
    bonus = jnp.sum(rh * kh * r_k, axis=-1, keepdims=True) * vh
    y = y + bonus.reshape(b, s, d)
    return (y * g) @ w_o


def short_conv_mixer(h, w_in, conv_w, w_out):
    bcu = h @ w_in
    gate_b, gate_c, u = jnp.split(bcu, 3, axis=-1)
    y = causal_dwconv(gate_c * u, conv_w)
    return (gate_b * y) @ w_out


def conv_ffn(h, w_gu, conv_w, conv_b, w_down):
    gate, up = jnp.split(h @ w_gu, 2, axis=-1)
    gate = causal_dwconv(gate, conv_w) + conv_b
    return (jax.nn.silu(gate) * up) @ w_down


def setup_inputs(seed: int = 0) -> dict:
    key = jax.random.key(seed)
    ks = iter(jax.random.split(key, 64))
    f32 = jnp.float32
    D, F = D_MODEL, D_FF

    def nrm(shape, scale):
        return jax.random.normal(next(ks), shape, f32) * scale

    def gain(shape):
        return 1.0 + nrm(shape, 0.02)

    nA, nB, nC = N_ATTN_LAYERS, N_RWKV_LAYERS, N_CONV_LAYERS
    inp = {}
    inp['x'] = nrm((BATCH, SEQ, D), 1.0)
    inp['p'] = nrm((DEPTH, BATCH, SEQ, PLE_DIM), 1.0)
    inp['attn_norm'] = gain((nA, D))
    inp['attn_w_qkv'] = nrm((nA, D, N_GROUPS * 3 * ATTN_HEADS * ATTN_HEAD_DIM), D ** -0.5)
    inp['attn_w_o'] = nrm((nA, ATTN_HEADS * ATTN_HEAD_DIM, D), (ATTN_HEADS * ATTN_HEAD_DIM) ** -0.5)
    inp['rwkv_norm'] = gain((nB, D))
    inp['rwkv_mu'] = jax.random.uniform(next(ks), (nB, 6, D), f32)
    inp['rwkv_w_rkv'] = nrm((nB, 3, D, D), D ** -0.5)
    inp['rwkv_w0'] = jax.random.uniform(next(ks), (nB, D), f32, -6.0, -1.0)
    inp['rwkv_w_w1'] = nrm((nB, D, RWKV_DECAY_LORA), D ** -0.5)
    inp['rwkv_w_w2'] = nrm((nB, RWKV_DECAY_LORA, D), RWKV_DECAY_LORA ** -0.5)
    inp['rwkv_a0'] = nrm((nB, D), 0.1)
    inp['rwkv_w_a1'] = nrm((nB, D, RWKV_A_LORA), D ** -0.5)
    inp['rwkv_w_a2'] = nrm((nB, RWKV_A_LORA, D), RWKV_A_LORA ** -0.5)
    inp['rwkv_w_g1'] = nrm((nB, D, RWKV_GATE_LORA), D ** -0.5)
    inp['rwkv_w_g2'] = nrm((nB, RWKV_GATE_LORA, D), RWKV_GATE_LORA ** -0.5)
    inp['rwkv_k_k'] = 0.85 + nrm((nB, D), 0.02)
    inp['rwkv_k_a'] = 1.0 + nrm((nB, D), 0.02)
    inp['rwkv_r_k'] = nrm((nB, RWKV_HEADS, RWKV_HEAD_SIZE), 0.1)
    inp['rwkv_ln_g'] = gain((nB, D))
    inp['rwkv_ln_b'] = nrm((nB, D), 0.01)
    inp['rwkv_w_o'] = nrm((nB, D, D), D ** -0.5)
    inp['conv_norm'] = gain((nC, D))
    inp['conv_w_in'] = nrm((nC, D, 3 * D), D ** -0.5)
    inp['conv_w'] = nrm((nC, CONV_WIDTH, D), CONV_WIDTH ** -0.5)
    inp['conv_w_out'] = nrm((nC, D, D), D ** -0.5)
    inp['ffn_norm'] = gain((DEPTH, D))
    inp['ffn_w_gu'] = nrm((DEPTH, D, 2 * F), D ** -0.5)
    inp['ffn_conv_w'] = nrm((DEPTH, FFN_CONV_WIDTH, F), FFN_CONV_WIDTH ** -0.5)
    inp['ffn_conv_b'] = nrm((DEPTH, F), 0.01)
    inp['ffn_w_down'] = nrm((DEPTH, F, D), F ** -0.5)
    inp['ple_w_proj'] = nrm((DEPTH, PLE_DIM, D), PLE_DIM ** -0.5)
    inp['ple_norm'] = gain((DEPTH, D))
    inp['ple_w_gate'] = nrm((DEPTH, D, D), D ** -0.5)
    inp['final_norm'] = gain((D,))
    return inp


def reference(x, p, attn_norm, attn_w_qkv, attn_w_o,
              rwkv_norm, rwkv_mu, rwkv_w_rkv, rwkv_w0, rwkv_w_w1, rwkv_w_w2,
              rwkv_a0, rwkv_w_a1, rwkv_w_a2, rwkv_w_g1, rwkv_w_g2,
              rwkv_k_k, rwkv_k_a, rwkv_r_k, rwkv_ln_g, rwkv_ln_b, rwkv_w_o,
              conv_norm, conv_w_in, conv_w, conv_w_out,
              ffn_norm, ffn_w_gu, ffn_conv_w, ffn_conv_b, ffn_w_down,
              ple_w_proj, ple_norm, ple_w_gate, final_norm):
    slopes = alibi_slopes()
    for i in range(DEPTH):
        kind, j = i % N_MIXERS, i // N_MIXERS
        if kind == 0:
            h = rms_norm(x, attn_norm[j])
            x = x + attention_mixer(h, attn_w_qkv[j], attn_w_o[j], slopes)
        elif kind == 1:
            h = rms_norm(x, rwkv_norm[j])
            x = x + rwkv7_mixer(h, rwkv_mu[j], rwkv_w_rkv[j], rwkv_w0[j], rwkv_w_w1[j], rwkv_w_w2[j],
                                rwkv_a0[j], rwkv_w_a1[j], rwkv_w_a2[j], rwkv_w_g1[j], rwkv_w_g2[j],
                                rwkv_k_k[j], rwkv_k_a[j], rwkv_r_k[j], rwkv_ln_g[j], rwkv_ln_b[j],
                                rwkv_w_o[j])
        else:
            h = rms_norm(x, conv_norm[j])
            x = x + short_conv_mixer(h, conv_w_in[j], conv_w[j], conv_w_out[j])
        h = rms_norm(x, ffn_norm[i])
        x = x + conv_ffn(h, ffn_w_gu[i], ffn_conv_w[i], ffn_conv_b[i], ffn_w_down[i])
        gate = jax.nn.sigmoid(rms_norm(x, ple_norm[i]) @ ple_w_gate[i])
        x = x + gate * (p[i] @ ple_w_proj[i])
    return rms_norm(x, final_norm)
```

```python
import functools

import jax
import jax.numpy as jnp
from jax import lax
from jax.experimental import pallas as pl
from jax.experimental.pallas import tpu as pltpu

F32 = jnp.float32
BF16 = jnp.bfloat16

NORM_EPS = 1e-6
NEG_INF = -1e30
RWKV_GN_EPS = 6.4e-4

ATTN_GROUPS = ((128, 1), (512, 4), (2048, 16))
ATTN_HEADS = 16
ATTN_HEAD_DIM = 128
ATTN_BLOCK = 128
RWKV_HEAD_SIZE = 64
SCAN_CHUNK = 64
LANES = 128
CARRY_ROWS = 8

VMEM_LIMIT_BYTES = 56 * 1024 * 1024


def _params(semantics):
    return pltpu.CompilerParams(dimension_semantics=semantics, vmem_limit_bytes=VMEM_LIMIT_BYTES)


def _sigmoid(x):
    return 1.0 / (1.0 + jnp.exp(-x))


def _rms(x, g):
    ms = jnp.mean(x * x, axis=-1, keepdims=True)
    return x * lax.rsqrt(ms + NORM_EPS) * g


def _shift_rows(cur, prev_tail, n):
    row = lax.broadcasted_iota(jnp.int32, cur.shape, 0)
    out = pltpu.roll(cur, n, axis=0)
    for r in range(n):
        src = CARRY_ROWS - n + r
        out = jnp.where(row == r, prev_tail[src:src + 1, :], out)
    return out


def _rmsnorm_body(x_ref, g_ref, o_ref):
    o_ref[...] = _rms(x_ref[...], g_ref[...]).astype(o_ref.dtype)


def rmsnorm(x, g, out_dtype, tm=256):
    m, d = x.shape
    return pl.pallas_call(
        _rmsnorm_body,
        out_shape=jax.ShapeDtypeStruct((m, d), out_dtype),
        grid=(m // tm,),
        in_specs=[pl.BlockSpec((tm, d), lambda i: (i, 0)),
                  pl.BlockSpec((1, d), lambda i: (0, 0))],
        out_specs=pl.BlockSpec((tm, d), lambda i: (i, 0)),
        compiler_params=_params(("arbitrary",)),
        name="rmsnorm",
    )(x, g.reshape(1, d))


def _attn_norm_body(x_ref, g_ref, o0_ref, o1_ref, o2_ref, y_ref, *, tm):
    y = _rms(x_ref[...], g_ref[...])
    o0_ref[...] = y.astype(o0_ref.dtype)
    for c in range(y_ref.shape[0]):
        y_ref[c] = y[:, c * LANES:(c + 1) * LANES]
    for o_ref, dil in ((o1_ref, ATTN_GROUPS[1][1]), (o2_ref, ATTN_GROUPS[2][1])):
        for r in range(dil):
            for c in range(y_ref.shape[0]):
                o_ref[r, :, c * LANES:(c + 1) * LANES] = (
                    y_ref[c, pl.ds(r, tm // dil, stride=dil), :].astype(o_ref.dtype))


def attn_rmsnorm(x, g, batch, seq, tm=256):
    m, d = x.shape
    bps = seq // tm
    d1, d2 = ATTN_GROUPS[1][1], ATTN_GROUPS[2][1]
    outs = pl.pallas_call(
        functools.partial(_attn_norm_body, tm=tm),
        out_shape=(jax.ShapeDtypeStruct((m, d), BF16),
                   jax.ShapeDtypeStruct((batch, d1, seq // d1, d), BF16),
                   jax.ShapeDtypeStruct((batch, d2, seq // d2, d), BF16)),
        grid=(batch, bps),
        in_specs=[pl.BlockSpec((tm, d), lambda b, i: (b * bps + i, 0)),
                  pl.BlockSpec((1, d), lambda b, i: (0, 0))],
        out_specs=(pl.BlockSpec((tm, d), lambda b, i: (b * bps + i, 0)),
                   pl.BlockSpec((None, d1, tm // d1, d), lambda b, i: (b, 0, i, 0)),
                   pl.BlockSpec((None, d2, tm // d2, d), lambda b, i: (b, 0, i, 0))),
        scratch_shapes=[pltpu.VMEM((d // LANES, tm, LANES), F32)],
        compiler_params=_params(("arbitrary", "arbitrary")),
        name="attn_norm",
    )(x, g.reshape(1, d))
    return jnp.stack([outs[0], outs[1].reshape(m, d), outs[2].reshape(m, d)], 0)


def _rwkv_prep_body(x_ref, g_ref, mu_ref, o_ref, carry_ref, *, bps):
    first = pl.program_id(0) % bps == 0

    @pl.when(first)
    def _():
        carry_ref[...] = jnp.zeros_like(carry_ref)

    h = _rms(x_ref[...], g_ref[...])
    xx = _shift_rows(h, carry_ref[...], 1) - h
    carry_ref[...] = h[h.shape[0] - CARRY_ROWS:, :]
    for n in range(o_ref.shape[0]):
        o_ref[n] = (h + xx * mu_ref[n:n + 1, :]).astype(o_ref.dtype)


def rwkv_prep(x, g, mu, seq, tm=256):
    m, d = x.shape
    n = mu.shape[0]
    return pl.pallas_call(
        functools.partial(_rwkv_prep_body, bps=seq // tm),
        out_shape=jax.ShapeDtypeStruct((n, m, d), BF16),
        grid=(m // tm,),
        in_specs=[pl.BlockSpec((tm, d), lambda i: (i, 0)),
                  pl.BlockSpec((1, d), lambda i: (0, 0)),
                  pl.BlockSpec((n, d), lambda i: (0, 0))],
        out_specs=pl.BlockSpec((n, tm, d), lambda i: (0, i, 0)),
        scratch_shapes=[pltpu.VMEM((CARRY_ROWS, d), F32)],
        compiler_params=_params(("arbitrary",)),
        name="rwkv_prep",
    )(x, g.reshape(1, d), mu)


def _cast_weight(w_ref, wb_ref, rows=256):
    k = w_ref.shape[0]
    rows = min(rows, k)

    def chunk(c, carry):
        sl = pl.ds(pl.multiple_of(c * rows, rows), rows)
        wb_ref[sl, :] = w_ref[sl, :].astype(wb_ref.dtype)
        return carry

    lax.fori_loop(0, k // rows, chunk, 0)


def _mm_body(*refs, n_x, w_lhs, n_extra, n_scratch, epilogue):
    n_w = len(w_lhs)
    x_refs = refs[:n_x]
    w_refs = refs[n_x:n_x + n_w]
    e_refs = refs[n_x + n_w:n_x + n_w + n_extra]
    o_ref = refs[n_x + n_w + n_extra]
    wb_refs = refs[n_x + n_w + n_extra + 1:n_x + 2 * n_w + n_extra + 1]
    s_refs = refs[n_x + 2 * n_w + n_extra + 1:]
    assert len(s_refs) == n_scratch

    @pl.when(pl.program_id(2) == 0)
    def _():
        for w_ref, wb_ref in zip(w_refs, wb_refs):
            _cast_weight(w_ref, wb_ref)

    xs = [x_ref[...].astype(BF16) for x_ref in x_refs]
    accs = [jnp.dot(xs[l], wb_ref[...], preferred_element_type=F32)
            for wb_ref, l in zip(wb_refs, w_lhs)]
    o_ref[...] = epilogue(accs, e_refs, s_refs).astype(o_ref.dtype)


def matmul(xs, ws, extras, epilogue, out_shape, out_spec, grid, scratch=(), name="matmul"):
    w_lhs = tuple(l for _, _, l in ws)
    wb_scratch = [pltpu.VMEM(spec.block_shape[-2:], BF16) for _, spec, _ in ws]
    body = functools.partial(_mm_body, n_x=len(xs), w_lhs=w_lhs, n_extra=len(extras),
                             n_scratch=len(scratch), epilogue=epilogue)
    return pl.pallas_call(
        body,
        out_shape=out_shape,
        grid=grid,
        in_specs=[s for _, s in xs] + [s for _, s, _ in ws] + [s for _, s in extras],
        out_specs=out_spec,
        scratch_shapes=wb_scratch + list(scratch),
        compiler_params=_params(("arbitrary", "arbitrary", "arbitrary")),
        name=name,
    )(*[a for a, _ in xs], *[a for a, _, _ in ws], *[a for a, _ in extras])


def _ep_plain(accs, e_refs, s_refs):
    return accs[0]


def _ep_residual(accs, e_refs, s_refs):
    return e_refs[0][...] + accs[0]


def _ep_ple(accs, e_refs, s_refs):
    return e_refs[0][...] + _sigmoid(accs[0]) * accs[1]


def _conv3(cur, cw_ref, carry_ref, bps):
    @pl.when(pl.program_id(2) % bps == 0)
    def _():
        carry_ref[...] = jnp.zeros_like(carry_ref)

    tail = carry_ref[...]
    d1 = _shift_rows(cur, tail, 1)
    d2 = _shift_rows(cur, tail, 2)
    carry_ref[...] = cur[cur.shape[0] - CARRY_ROWS:, :]
    w = cw_ref[...]
    return d2 * w[0:1, :] + d1 * w[1:2, :] + cur * w[2:3, :]


def _ep_ffn(accs, e_refs, s_refs, *, bps):
    gate = _conv3(accs[0], e_refs[0], s_refs[0], bps) + e_refs[1][...]
    return gate * _sigmoid(gate) * accs[1]


def _ep_shortconv(accs, e_refs, s_refs, *, bps):
    return accs[0] * _conv3(accs[1] * accs[2], e_refs[0], s_refs[0], bps)


def linear(x, w, out_dtype, tm=512, tn=1024, residual=None, name="linear"):
    m, k = x.shape
    n = w.shape[1]
    tn = min(tn, n)
    extras = []
    epilogue = _ep_plain
    if residual is not None:
        extras = [(residual, pl.BlockSpec((tm, tn), lambda g, j, i: (i, j)))]
        epilogue = _ep_residual
    return matmul(
        [(x, pl.BlockSpec((tm, k), lambda g, j, i: (i, 0)))],
        [(w, pl.BlockSpec((k, tn), lambda g, j, i: (0, j)), 0)],
        extras, epilogue,
        jax.ShapeDtypeStruct((m, n), out_dtype),
        pl.BlockSpec((tm, tn), lambda g, j, i: (i, j)),
        (1, n // tn, m // tm), name=name)


def grouped_linear(x, w, n_groups, w_group_stride, n_out, out_dtype, tm=512, tn=1024, name="grouped_linear"):
    m, k = x.shape[1:]
    tn = min(tn, n_out)
    nj = n_out // tn
    if w.ndim == 2:
        w_spec = pl.BlockSpec((k, tn), lambda g, j, i: (0, g * nj + j))
    else:
        w_spec = pl.BlockSpec((None, k, tn), lambda g, j, i: (g, 0, j))
    return matmul(
        [(x, pl.BlockSpec((None, tm, k), lambda g, j, i: (g, i, 0)))],
        [(w, w_spec, 0)],
        [], _ep_plain,
        jax.ShapeDtypeStruct((n_groups, m, n_out), out_dtype),
        pl.BlockSpec((None, tm, tn), lambda g, j, i: (g, i, j)),
        (n_groups, nj, m // tm), name=name)


def ffn_gate_up(h, w_gu, conv_w, conv_b, seq, tm=512, tn=512):
    m, k = h.shape
    f = w_gu.shape[1] // 2
    nj = f // tn
    vec = lambda rows: pl.BlockSpec((rows, tn), lambda g, j, i: (0, j))
    return matmul(
        [(h, pl.BlockSpec((tm, k), lambda g, j, i: (i, 0)))],
        [(w_gu, pl.BlockSpec((k, tn), lambda g, j, i: (0, j)), 0),
         (w_gu, pl.BlockSpec((k, tn), lambda g, j, i: (0, nj + j)), 0)],
        [(conv_w, vec(conv_w.shape[0])), (conv_b.reshape(1, f), vec(1))],
        functools.partial(_ep_ffn, bps=seq // tm),
        jax.ShapeDtypeStruct((m, f), BF16),
        pl.BlockSpec((tm, tn), lambda g, j, i: (i, j)),
        (1, nj, m // tm),
        scratch=[pltpu.VMEM((CARRY_ROWS, tn), F32)], name="ffn_gate_up")


def shortconv_in(h, w_in, conv_w, seq, tm=512, tn=512):
    m, k = h.shape
    d = w_in.shape[1] // 3
    nj = d // tn
    return matmul(
        [(h, pl.BlockSpec((tm, k), lambda g, j, i: (i, 0)))],
        [(w_in, pl.BlockSpec((k, tn), lambda g, j, i, s=s: (0, s * nj + j)), 0) for s in range(3)],
        [(conv_w, pl.BlockSpec((conv_w.shape[0], tn), lambda g, j, i: (0, j)))],
        functools.partial(_ep_shortconv, bps=seq // tm),
        jax.ShapeDtypeStruct((m, d), BF16),
        pl.BlockSpec((tm, tn), lambda g, j, i: (i, j)),
        (1, nj, m // tm),
        scratch=[pltpu.VMEM((CARRY_ROWS, tn), F32)], name="shortconv_in")


def ple_update(x, hn, p, w_gate, w_proj, tm=512, tn=1024):
    m, d = x.shape
    kp = p.shape[1]
    return matmul(
        [(hn, pl.BlockSpec((tm, d), lambda g, j, i: (i, 0))),
         (p, pl.BlockSpec((tm, kp), lambda g, j, i: (i, 0)))],
        [(w_gate, pl.BlockSpec((d, tn), lambda g, j, i: (0, j)), 0),
         (w_proj, pl.BlockSpec((kp, tn), lambda g, j, i: (0, j)), 1)],
        [(x, pl.BlockSpec((tm, tn), lambda g, j, i: (i, j)))],
        _ep_ple,
        jax.ShapeDtypeStruct((m, d), F32),
        pl.BlockSpec((tm, tn), lambda g, j, i: (i, j)),
        (1, d // tn, m // tm), name="ple_update")


def _lora_body(x_ref, w1_ref, w2_ref, b_ref, o_ref, w1b_ref, w2b_ref, *, mode):
    @pl.when(pl.program_id(0) == 0)
    def _():
        _cast_weight(w1_ref, w1b_ref)
        _cast_weight(w2_ref, w2b_ref)

    mid = jnp.dot(x_ref[...], w1b_ref[...], preferred_element_type=F32)
    if mode == "decay":
        mid = jnp.tanh(mid)
    elif mode == "gate":
        mid = _sigmoid(mid)
    z = jnp.dot(mid.astype(BF16), w2b_ref[...], preferred_element_type=F32)
    if mode == "decay":
        u = -(b_ref[...] + z)
        softplus = jnp.maximum(u, 0.0) + jnp.log1p(jnp.exp(-jnp.abs(u)))
        z = -jnp.exp(-softplus - 0.5)
    elif mode == "iclr":
        z = _sigmoid(b_ref[...] + z)
    o_ref[...] = z


def lora(xs, idx, w1, w2, bias, mode, tm=512):
    n, m, d = xs.shape
    r = -(-w1.shape[1] // LANES) * LANES
    w1 = jnp.pad(w1, ((0, 0), (0, r - w1.shape[1])))
    w2 = jnp.pad(w2, ((0, r - w2.shape[0]), (0, 0)))
    return pl.pallas_call(
        functools.partial(_lora_body, mode=mode),
        out_shape=jax.ShapeDtypeStruct((m, d), F32),
        grid=(m // tm,),
        in_specs=[pl.BlockSpec((None, tm, d), lambda i: (idx, i, 0)),
                  pl.BlockSpec((d, r), lambda i: (0, 0)),
                  pl.BlockSpec((r, d), lambda i: (0, 0)),
                  pl.BlockSpec((1, d), lambda i: (0, 0))],
        out_specs=pl.BlockSpec((tm, d), lambda i: (i, 0)),
        scratch_shapes=[pltpu.VMEM((d, r), BF16), pltpu.VMEM((r, d), BF16)],
        compiler_params=_params(("arbitrary",)),
        name="lora_" + mode,
    )(xs, w1, w2, bias.reshape(1, d))


def _attn_body(slopes_ref, *refs):
    qkv_refs = refs[:9]
    o_ref = refs[9]
    acc_ref, m_ref, l_ref = refs[10:]
    head = pl.program_id(1)
    blk = ATTN_BLOCK
    scale = ATTN_HEAD_DIM ** -0.5

    qi = lax.broadcasted_iota(jnp.int32, (blk, 2 * blk), 0)
    kj = lax.broadcasted_iota(jnp.int32, (blk, 2 * blk), 1)
    dist = qi + blk - kj
    distf = dist.astype(F32)

    for g, (window, dil) in enumerate(ATTN_GROUPS):
        q_ref, k_ref, v_ref = qkv_refs[3 * g:3 * g + 3]
        seq = q_ref.shape[0]
        nb = seq // dil // blk
        n_back = window // dil
        coef = slopes_ref[g, head] * float(dil)
        bias2 = jnp.where((dist >= 0) & (dist <= n_back), -(coef * distf), NEG_INF)
        bias1 = bias2[:, blk:]
        for j in range(seq // blk):
            r, n = j // nb, j % nb
            q = q_ref[j * blk:(j + 1) * blk, :]
            lo = j * blk if n == 0 else (j - 1) * blk
            k = k_ref[lo:(j + 1) * blk, :]
            v = v_ref[lo:(j + 1) * blk, :]
            s = lax.dot_general(q, k, (((1,), (1,)), ((), ())), preferred_element_type=F32)
            s = s * scale + (bias1 if n == 0 else bias2)
            m = jnp.max(s, axis=1, keepdims=True)
            p = jnp.exp(s - m)
            l = jnp.sum(p, axis=1, keepdims=True)
            acc = jnp.dot(p.astype(BF16), v, preferred_element_type=F32)
            start = n * blk * dil + r
            rows = pl.ds(start, blk) if dil == 1 else pl.ds(start, blk, stride=dil)
            acc_ref[g, rows, :] = acc
            m_ref[g, rows, :] = jnp.broadcast_to(m, acc.shape)
            l_ref[g, rows, :] = jnp.broadcast_to(l, acc.shape)

    rows_per_step = 256
    for c in range(o_ref.shape[0] // rows_per_step):
        sl = pl.ds(c * rows_per_step, rows_per_step)
        ms = [m_ref[g, sl, :] for g in range(3)]
        top = jnp.maximum(jnp.maximum(ms[0], ms[1]), ms[2])
        es = [jnp.exp(mg - top) for mg in ms]
        num = es[0] * acc_ref[0, sl, :] + es[1] * acc_ref[1, sl, :] + es[2] * acc_ref[2, sl, :]
        den = es[0] * l_ref[0, sl, :] + es[1] * l_ref[1, sl, :] + es[2] * l_ref[2, sl, :]
        o_ref[sl, :] = (num / den).astype(o_ref.dtype)


def dilated_attention(qkv, slopes, batch, seq):
    n_g, m, _ = qkv.shape
    hd = ATTN_HEAD_DIM
    in_specs = [pl.BlockSpec(memory_space=pltpu.SMEM)]
    for g in range(n_g):
        for c in range(3):
            in_specs.append(pl.BlockSpec((None, seq, hd), lambda b, h, g=g, c=c: (g, b, c * ATTN_HEADS + h)))
    return pl.pallas_call(
        _attn_body,
        out_shape=jax.ShapeDtypeStruct((m, ATTN_HEADS * hd), BF16),
        grid=(batch, ATTN_HEADS),
        in_specs=in_specs,
        out_specs=pl.BlockSpec((seq, hd), lambda b, h: (b, h)),
        scratch_shapes=[pltpu.VMEM((n_g, seq, hd), F32)] * 3,
        compiler_params=_params(("arbitrary", "arbitrary")),
        name="dilated_attention",
    )(slopes, *([qkv] * 9))


def _split3(x):
    hi = x.astype(BF16)
    r1 = x - hi.astype(F32)
    mid = r1.astype(BF16)
    lo = (r1 - mid.astype(F32)).astype(BF16)
    return hi, mid, lo


def _dot_exact_rhs(x, rhs_bf16):
    return sum(jnp.dot(part, rhs_bf16, preferred_element_type=F32) for part in _split3(x))


def _dot_exact_lhs(lhs_bf16, x):
    return sum(jnp.dot(lhs_bf16, part, preferred_element_type=F32) for part in _split3(x))


def _dot_t(a, b):
    return lax.dot_general(a, b, (((1,), (1,)), ((), ())), preferred_element_type=F32)


def _t_dot(a, b):
    return lax.dot_general(a, b, (((0,), (0,)), ((), ())), preferred_element_type=F32)


def _scan_body(r_ref, k_ref, v_ref, a_ref, lw_ref, g_ref, kk_ref, ka_ref, rk_ref, lng_ref, lnb_ref,
               o_ref, state_ref, *, pairs_per_step):
    t = r_ref.shape[0]
    n_pairs = r_ref.shape[1] // LANES

    @pl.when(pl.program_id(1) == 0)
    def _():
        state_ref[...] = jnp.zeros_like(state_ref)

    lane = lax.broadcasted_iota(jnp.int32, (t, LANES), 1)
    head0 = lane < RWKV_HEAD_SIZE
    ri = lax.broadcasted_iota(jnp.int32, (2 * t, 2 * t), 0)
    ci = lax.broadcasted_iota(jnp.int32, (2 * t, 2 * t), 1)
    same_head = (ri < t) == (ci < t)
    strict = same_head & (ci < ri)
    incl = same_head & (ci <= ri)
    eye = jnp.where(ri == ci, 1.0, 0.0)
    tri = jnp.where(lax.broadcasted_iota(jnp.int32, (t, t), 1) <= lax.broadcasted_iota(jnp.int32, (t, t), 0),
                    1.0, 0.0).astype(BF16)
    li = lax.broadcasted_iota(jnp.int32, (LANES, LANES), 0)
    lj = lax.broadcasted_iota(jnp.int32, (LANES, LANES), 1)
    head_ones = jnp.where((li < RWKV_HEAD_SIZE) == (lj < RWKV_HEAD_SIZE), 1.0, 0.0).astype(BF16)
    n_doublings = (t - 1).bit_length()

    def stack(x):
        return jnp.concatenate([jnp.where(head0, x, 0.0), jnp.where(head0, 0.0, x)], axis=0)

    def dup(x):
        return jnp.concatenate([x, x], axis=0)

    def head_sum(x):
        return _dot_exact_rhs(x, head_ones)

    def one_pair(pr):
        sl = pl.ds(pl.multiple_of(pr * LANES, LANES), LANES)
        r = r_ref[:, sl]
        k0 = k_ref[:, sl]
        v = v_ref[:, sl]
        a_gate = a_ref[:, sl]
        lw = lw_ref[:, sl]

        kk = k0 * kk_ref[:, sl]
        kk = kk / jnp.maximum(jnp.sqrt(head_sum(kk * kk)), 1e-12)
        k = k0 * (1.0 + (a_gate - 1.0) * ka_ref[:, sl])
        a = -kk
        b = kk * a_gate

        lg = _dot_exact_lhs(tri, lw)
        g_incl = jnp.exp(lg)
        g_excl = jnp.exp(lg - lw)
        g_inv = jnp.exp(-lg)
        g_last = g_incl[t - 1:t, :]
        g_tail = g_last * g_inv

        a_s = stack(a * g_excl).astype(BF16)
        r_s = stack(r * g_incl).astype(BF16)
        v_s = stack(v).astype(BF16)
        bt = dup(b * g_inv).astype(BF16)
        kt = dup(k * g_inv).astype(BF16)
        bh_s = stack(b * g_tail).astype(BF16)
        kh_s = stack(k * g_tail).astype(BF16)

        n_ab = jnp.where(strict, _dot_t(a_s, bt), 0.0)
        n_ak = jnp.where(strict, _dot_t(a_s, kt), 0.0)
        b_rb = jnp.where(incl, _dot_t(r_s, bt), 0.0)
        b_rk = jnp.where(incl, _dot_t(r_s, kt), 0.0)

        inv = eye + n_ab
        pw = n_ab
        for _ in range(n_doublings - 1):
            pwb = pw.astype(BF16)
            pw = jnp.dot(pwb, pwb, preferred_element_type=F32)
            inv = inv + jnp.dot(pw.astype(BF16), inv.astype(BF16), preferred_element_type=F32)

        state = state_ref[pr]
        state_b = state.astype(BF16)
        w = _dot_t(a_s, state_b) + jnp.dot(n_ak.astype(BF16), v_s, preferred_element_type=F32)
        z = jnp.dot(inv.astype(BF16), w.astype(BF16), preferred_element_type=F32)
        z_b = z.astype(BF16)
        y_s = (_dot_t(r_s, state_b) + jnp.dot(b_rb.astype(BF16), z_b, preferred_element_type=F32)
               + jnp.dot(b_rk.astype(BF16), v_s, preferred_element_type=F32))
        y = y_s[:t, :] + y_s[t:, :]
        state_ref[pr] = state * g_last + _t_dot(z_b, bh_s) + _t_dot(v_s, kh_s)

        inv_n = 1.0 / RWKV_HEAD_SIZE
        mean = head_sum(y) * inv_n
        yc = y - mean
        var = head_sum(yc * yc) * inv_n
        yn = yc * lax.rsqrt(var + RWKV_GN_EPS) * lng_ref[:, sl] + lnb_ref[:, sl]
        bonus = head_sum(r * k * rk_ref[:, sl]) * v
        o_ref[:, sl] = ((yn + bonus) * g_ref[:, sl]).astype(o_ref.dtype)

    def step(i, carry):
        for u in range(pairs_per_step):
            one_pair(i * pairs_per_step + u)
        return carry

    lax.fori_loop(0, n_pairs // pairs_per_step, step, 0)


def rwkv_scan(rkv, a_gate, log_w, gate, k_k, k_a, r_k, ln_g, ln_b, batch, seq, pairs_per_step=4):
    _, m, d = rkv.shape
    t = SCAN_CHUNK
    nc = seq // t
    row = lambda g: pl.BlockSpec((None, t, d), lambda b, c, g=g: (g, b * nc + c, 0))
    tile = pl.BlockSpec((t, d), lambda b, c: (b * nc + c, 0))
    vec = pl.BlockSpec((1, d), lambda b, c: (0, 0))
    return pl.pallas_call(
        functools.partial(_scan_body, pairs_per_step=pairs_per_step),
        out_shape=jax.ShapeDtypeStruct((m, d), BF16),
        grid=(batch, nc),
        in_specs=[row(0), row(1), row(2), tile, tile, tile, vec, vec, vec, vec, vec],
        out_specs=tile,
        scratch_shapes=[pltpu.VMEM((d // LANES, LANES, LANES), F32)],
        compiler_params=_params(("arbitrary", "arbitrary")),
        name="rwkv_scan",
    )(rkv, rkv, rkv, a_gate, log_w, gate,
      k_k.reshape(1, d), k_a.reshape(1, d), r_k.reshape(1, d), ln_g.reshape(1, d), ln_b.reshape(1, d))


def _alibi_slopes(n_groups, n_heads):
    n = n_groups * n_heads
    idx = jnp.arange(1, n + 1, dtype=F32)
    return (2.0 ** (-8.0 * idx / n)).reshape(n_groups, n_heads)


def kernel(x, p, attn_norm, attn_w_qkv, attn_w_o, rwkv_norm, rwkv_mu, rwkv_w_rkv, rwkv_w0, rwkv_w_w1, rwkv_w_w2, rwkv_a0, rwkv_w_a1, rwkv_w_a2, rwkv_w_g1, rwkv_w_g2, rwkv_k_k, rwkv_k_a, rwkv_r_k, rwkv_ln_g, rwkv_ln_b, rwkv_w_o, conv_norm, conv_w_in, conv_w, conv_w_out, ffn_norm, ffn_w_gu, ffn_conv_w, ffn_conv_b, ffn_w_down, ple_w_proj, ple_norm, ple_w_gate, final_norm):
    batch, seq, d = x.shape
    depth = p.shape[0]
    m = batch * seq
    n_groups = len(ATTN_GROUPS)
    slopes = _alibi_slopes(n_groups, ATTN_HEADS)
    x = x.reshape(m, d)
    p = p.reshape(depth, m, p.shape[-1])

    for i in range(depth):
        kind, j = i % 3, i // 3
        if kind == 0:
            hs = attn_rmsnorm(x, attn_norm[j], batch, seq)
            qkv = grouped_linear(hs, attn_w_qkv[j], n_groups, None, 3 * ATTN_HEADS * ATTN_HEAD_DIM, BF16,
                                 name="attn_qkv")
            o = dilated_attention(qkv, slopes, batch, seq)
            x = linear(o, attn_w_o[j], F32, residual=x, name="attn_out")
        elif kind == 1:
            xs = rwkv_prep(x, rwkv_norm[j], rwkv_mu[j], seq)
            rkv = grouped_linear(xs, rwkv_w_rkv[j], 3, 1, d, F32, name="rwkv_rkv")
            log_w = lora(xs, 3, rwkv_w_w1[j], rwkv_w_w2[j], rwkv_w0[j], "decay")
            a_gate = lora(xs, 4, rwkv_w_a1[j], rwkv_w_a2[j], rwkv_a0[j], "iclr")
            gate = lora(xs, 5, rwkv_w_g1[j], rwkv_w_g2[j], jnp.zeros((d,), F32), "gate")
            y = rwkv_scan(rkv, a_gate, log_w, gate, rwkv_k_k[j], rwkv_k_a[j], rwkv_r_k[j].reshape(d),
                          rwkv_ln_g[j], rwkv_ln_b[j], batch, seq)
            x = linear(y, rwkv_w_o[j], F32, residual=x, name="rwkv_out")
        else:
            h = rmsnorm(x, conv_norm[j], BF16)
            y = shortconv_in(h, conv_w_in[j], conv_w[j], seq)
            x = linear(y, conv_w_out[j], F32, residual=x, name="conv_out")
        h = rmsnorm(x, ffn_norm[i], BF16)
        act = ffn_gate_up(h, ffn_w_gu[i], ffn_conv_w[i], ffn_conv_b[i], seq)
        x = linear(act, ffn_w_down[i], F32, tm=512, tn=512, residual=x, name="ffn_down")
        hn = rmsnorm(x, ple_norm[i], BF16)
        x = ple_update(x, hn, p[i], ple_w_gate[i], ple_w_proj[i])
    return rmsnorm(x, final_norm, F32).reshape(batch, seq, d)
```

```python
import functools

import jax
import jax.numpy as jnp
from jax import lax
from jax.experimental import pallas as pl
from jax.experimental.pallas import tpu as pltpu

F32 = jnp.float32
BF16 = jnp.bfloat16

NORM_EPS = 1e-6
NEG_INF = -1e30
RWKV_GN_EPS = 6.4e-4

ATTN_GROUPS = ((128, 1), (512, 4), (2048, 16))
ATTN_HEADS = 16
ATTN_HEAD_DIM = 128
ATTN_BLOCK = 128
RWKV_HEAD_SIZE = 64
SCAN_CHUNK = 64
LANES = 128
CARRY_ROWS = 8

VMEM_LIMIT_BYTES = 56 * 1024 * 1024


def _params(semantics):
    return pltpu.CompilerParams(dimension_semantics=semantics, vmem_limit_bytes=VMEM_LIMIT_BYTES)


def _sigmoid(x):
    return 1.0 / (1.0 + jnp.exp(-x))


def _rms(x, g):
    ms = jnp.mean(x * x, axis=-1, keepdims=True)
    return x * lax.rsqrt(ms + NORM_EPS) * g


def _shift_rows(cur, prev_tail, n):
    row = lax.broadcasted_iota(jnp.int32, cur.shape, 0)
    out = pltpu.roll(cur, n, axis=0)
    for r in range(n):
        src = CARRY_ROWS - n + r
        out = jnp.where(row == r, prev_tail[src:src + 1, :], out)
    return out


def _rmsnorm_body(x_ref, g_ref, o_ref):
    o_ref[...] = _rms(x_ref[...], g_ref[...]).astype(o_ref.dtype)


def rmsnorm(x, g, out_dtype, tm=256):
    m, d = x.shape
    return pl.pallas_call(
        _rmsnorm_body,
        out_shape=jax.ShapeDtypeStruct((m, d), out_dtype),
        grid=(m // tm,),
        in_specs=[pl.BlockSpec((tm, d), lambda i: (i, 0)),
                  pl.BlockSpec((1, d), lambda i: (0, 0))],
        out_specs=pl.BlockSpec((tm, d), lambda i: (i, 0)),
        compiler_params=_params(("arbitrary",)),
        name="rmsnorm",
    )(x, g.reshape(1, d))


def _attn_norm_body(x_ref, g_ref, o0_ref, o1_ref, o2_ref, y_ref, *, tm):
    y = _rms(x_ref[...], g_ref[...])
    o0_ref[...] = y.astype(o0_ref.dtype)
    for c in range(y_ref.shape[0]):
        y_ref[c] = y[:, c * LANES:(c + 1) * LANES]
    for o_ref, dil in ((o1_ref, ATTN_GROUPS[1][1]), (o2_ref, ATTN_GROUPS[2][1])):
        for r in range(dil):
            for c in range(y_ref.shape[0]):
                o_ref[r, :, c * LANES:(c + 1) * LANES] = (
                    y_ref[c, pl.ds(r, tm // dil, stride=dil), :].astype(o_ref.dtype))


def attn_rmsnorm(x, g, batch, seq, tm=256):
    m, d = x.shape
    bps = seq // tm
    d1, d2 = ATTN_GROUPS[1][1], ATTN_GROUPS[2][1]
    outs = pl.pallas_call(
        functools.partial(_attn_norm_body, tm=tm),
        out_shape=(jax.ShapeDtypeStruct((m, d), BF16),
                   jax.ShapeDtypeStruct((batch, d1, seq // d1, d), BF16),
                   jax.ShapeDtypeStruct((batch, d2, seq // d2, d), BF16)),
        grid=(batch, bps),
        in_specs=[pl.BlockSpec((tm, d), lambda b, i: (b * bps + i, 0)),
                  pl.BlockSpec((1, d), lambda b, i: (0, 0))],
        out_specs=(pl.BlockSpec((tm, d), lambda b, i: (b * bps + i, 0)),
                   pl.BlockSpec((None, d1, tm // d1, d), lambda b, i: (b, 0, i, 0)),
                   pl.BlockSpec((None, d2, tm // d2, d), lambda b, i: (b, 0, i, 0))),
        scratch_shapes=[pltpu.VMEM((d // LANES, tm, LANES), F32)],
        compiler_params=_params(("arbitrary", "arbitrary")),
        name="attn_norm",
    )(x, g.reshape(1, d))
    return [o.reshape(m, d) for o in outs]


def _rwkv_prep_body(x_ref, g_ref, mu_ref, o_ref, carry_ref, *, bps):
    first = pl.program_id(0) % bps == 0

    @pl.when(first)
    def _():
        carry_ref[...] = jnp.zeros_like(carry_ref)

    h = _rms(x_ref[...], g_ref[...])
    xx = _shift_rows(h, carry_ref[...], 1) - h
    carry_ref[...] = h[h.shape[0] - CARRY_ROWS:, :]
    for n in range(o_ref.shape[0]):
        o_ref[n] = (h + xx * mu_ref[n:n + 1, :]).astype(o_ref.dtype)


def rwkv_prep(x, g, mu, seq, tm=256):
    m, d = x.shape
    n = mu.shape[0]
    return pl.pallas_call(
        functools.partial(_rwkv_prep_body, bps=seq // tm),
        out_shape=jax.ShapeDtypeStruct((n, m, d), BF16),
        grid=(m // tm,),
        in_specs=[pl.BlockSpec((tm, d), lambda i: (i, 0)),
                  pl.BlockSpec((1, d), lambda i: (0, 0)),
                  pl.BlockSpec((n, d), lambda i: (0, 0))],
        out_specs=pl.BlockSpec((n, tm, d), lambda i: (0, i, 0)),
        scratch_shapes=[pltpu.VMEM((CARRY_ROWS, d), F32)],
        compiler_params=_params(("arbitrary",)),
        name="rwkv_prep",
    )(x, g.reshape(1, d), mu)


def _cast_weight(w_ref, wb_ref, rows=256):
    k = w_ref.shape[0]
    rows = min(rows, k)

    def chunk(c, carry):
        sl = pl.ds(pl.multiple_of(c * rows, rows), rows)
        wb_ref[sl, :] = w_ref[sl, :].astype(wb_ref.dtype)
        return carry

    lax.fori_loop(0, k // rows, chunk, 0)


def _mm_body(*refs, n_x, w_lhs, n_extra, n_scratch, epilogue):
    n_w = len(w_lhs)
    x_refs = refs[:n_x]
    w_refs = refs[n_x:n_x + n_w]
    e_refs = refs[n_x + n_w:n_x + n_w + n_extra]
    o_ref = refs[n_x + n_w + n_extra]
    wb_refs = refs[n_x + n_w + n_extra + 1:n_x + 2 * n_w + n_extra + 1]
    s_refs = refs[n_x + 2 * n_w + n_extra + 1:]
    assert len(s_refs) == n_scratch

    @pl.when(pl.program_id(2) == 0)
    def _():
        for w_ref, wb_ref in zip(w_refs, wb_refs):
            _cast_weight(w_ref, wb_ref)

    xs = [x_ref[...].astype(BF16) for x_ref in x_refs]
    accs = [jnp.dot(xs[l], wb_ref[...], preferred_element_type=F32)
            for wb_ref, l in zip(wb_refs, w_lhs)]
    o_ref[...] = epilogue(accs, e_refs, s_refs).astype(o_ref.dtype)


def matmul(xs, ws, extras, epilogue, out_shape, out_spec, grid, scratch=(), name="matmul"):
    w_lhs = tuple(l for _, _, l in ws)
    wb_scratch = [pltpu.VMEM(spec.block_shape[-2:], BF16) for _, spec, _ in ws]
    body = functools.partial(_mm_body, n_x=len(xs), w_lhs=w_lhs, n_extra=len(extras),
                             n_scratch=len(scratch), epilogue=epilogue)
    return pl.pallas_call(
        body,
        out_shape=out_shape,
        grid=grid,
        in_specs=[s for _, s in xs] + [s for _, s, _ in ws] + [s for _, s in extras],
        out_specs=out_spec,
        scratch_shapes=wb_scratch + list(scratch),
        compiler_params=_params(("arbitrary", "arbitrary", "arbitrary")),
        name=name,
    )(*[a for a, _ in xs], *[a for a, _, _ in ws], *[a for a, _ in extras])


def _ep_plain(accs, e_refs, s_refs):
    return accs[0]


def _ep_residual(accs, e_refs, s_refs):
    return e_refs[0][...] + accs[0]


def _ep_ple(accs, e_refs, s_refs):
    return e_refs[0][...] + _sigmoid(accs[0]) * accs[1]


def _conv3(cur, cw_ref, carry_ref, bps):
    @pl.when(pl.program_id(2) % bps == 0)
    def _():
        carry_ref[...] = jnp.zeros_like(carry_ref)

    tail = carry_ref[...]
    d1 = _shift_rows(cur, tail, 1)
    d2 = _shift_rows(cur, tail, 2)
    carry_ref[...] = cur[cur.shape[0] - CARRY_ROWS:, :]
    w = cw_ref[...]
    return d2 * w[0:1, :] + d1 * w[1:2, :] + cur * w[2:3, :]


def _ep_ffn(accs, e_refs, s_refs, *, bps):
    gate = _conv3(accs[0], e_refs[0], s_refs[0], bps) + e_refs[1][...]
    return gate * _sigmoid(gate) * accs[1]


def _ep_shortconv(accs, e_refs, s_refs, *, bps):
    return accs[0] * _conv3(accs[1] * accs[2], e_refs[0], s_refs[0], bps)


def _layer_cols(w, layer, tn, first_block=0):
    k = w.shape[1]
    return pl.BlockSpec((None, k, tn), lambda g, j, i: (layer, 0, first_block + j))


def linear(x, w, layer, out_dtype, tm=512, tn=1024, n_out=None, first_col=0, residual=None, name="linear"):
    m, k = x.shape
    n = n_out or w.shape[2]
    tn = min(tn, n)
    extras = []
    epilogue = _ep_plain
    if residual is not None:
        extras = [(residual, pl.BlockSpec((tm, tn), lambda g, j, i: (i, j)))]
        epilogue = _ep_residual
    return matmul(
        [(x, pl.BlockSpec((tm, k), lambda g, j, i: (i, 0)))],
        [(w, _layer_cols(w, layer, tn, first_col // tn), 0)],
        extras, epilogue,
        jax.ShapeDtypeStruct((m, n), out_dtype),
        pl.BlockSpec((tm, tn), lambda g, j, i: (i, j)),
        (1, n // tn, m // tm), name=name)


def grouped_linear(x, w, layer, out_dtype, tm=512, tn=1024, name="grouped_linear"):
    m, k = x.shape[1:]
    n_groups, _, n_out = w.shape[1:]
    tn = min(tn, n_out)
    return matmul(
        [(x, pl.BlockSpec((None, tm, k), lambda g, j, i: (g, i, 0)))],
        [(w, pl.BlockSpec((None, None, k, tn), lambda g, j, i: (layer, g, 0, j)), 0)],
        [], _ep_plain,
        jax.ShapeDtypeStruct((n_groups, m, n_out), out_dtype),
        pl.BlockSpec((None, tm, tn), lambda g, j, i: (g, i, j)),
        (n_groups, n_out // tn, m // tm), name=name)


def ffn_gate_up(h, w_gu, layer, conv_w, conv_b, seq, tm=512, tn=512):
    m, k = h.shape
    f = w_gu.shape[2] // 2
    nj = f // tn
    vec = lambda rows: pl.BlockSpec((rows, tn), lambda g, j, i: (0, j))
    return matmul(
        [(h, pl.BlockSpec((tm, k), lambda g, j, i: (i, 0)))],
        [(w_gu, _layer_cols(w_gu, layer, tn), 0),
         (w_gu, _layer_cols(w_gu, layer, tn, nj), 0)],
        [(conv_w, vec(conv_w.shape[0])), (conv_b.reshape(1, f), vec(1))],
        functools.partial(_ep_ffn, bps=seq // tm),
        jax.ShapeDtypeStruct((m, f), BF16),
        pl.BlockSpec((tm, tn), lambda g, j, i: (i, j)),
        (1, nj, m // tm),
        scratch=[pltpu.VMEM((CARRY_ROWS, tn), F32)], name="ffn_gate_up")


def shortconv_in(h, w_in, layer, conv_w, seq, tm=512, tn=512):
    m, k = h.shape
    d = w_in.shape[2] // 3
    nj = d // tn
    return matmul(
        [(h, pl.BlockSpec((tm, k), lambda g, j, i: (i, 0)))],
        [(w_in, _layer_cols(w_in, layer, tn, s * nj), 0) for s in range(3)],
        [(conv_w, pl.BlockSpec((conv_w.shape[0], tn), lambda g, j, i: (0, j)))],
        functools.partial(_ep_shortconv, bps=seq // tm),
        jax.ShapeDtypeStruct((m, d), BF16),
        pl.BlockSpec((tm, tn), lambda g, j, i: (i, j)),
        (1, nj, m // tm),
        scratch=[pltpu.VMEM((CARRY_ROWS, tn), F32)], name="shortconv_in")


def ple_update(x, hn, p, layer, w_gate, w_proj, tm=512, tn=1024):
    m, d = x.shape
    kp = p.shape[2]
    return matmul(
        [(hn, pl.BlockSpec((tm, d), lambda g, j, i: (i, 0))),
         (p, pl.BlockSpec((None, tm, kp), lambda g, j, i: (layer, i, 0)))],
        [(w_gate, _layer_cols(w_gate, layer, tn), 0),
         (w_proj, _layer_cols(w_proj, layer, tn), 1)],
        [(x, pl.BlockSpec((tm, tn), lambda g, j, i: (i, j)))],
        _ep_ple,
        jax.ShapeDtypeStruct((m, d), F32),
        pl.BlockSpec((tm, tn), lambda g, j, i: (i, j)),
        (1, d // tn, m // tm), name="ple_update")


def _lora_body(x_ref, w1_ref, w2_ref, *refs, mode):
    b_ref = refs[0] if mode != "gate" else None
    o_ref, w1b_ref, w2b_ref = refs[-3:]

    @pl.when(pl.program_id(0) == 0)
    def _():
        _cast_weight(w1_ref, w1b_ref)
        _cast_weight(w2_ref, w2b_ref)

    mid = jnp.dot(x_ref[...], w1b_ref[...], preferred_element_type=F32)
    if mode == "decay":
        mid = jnp.tanh(mid)
    elif mode == "gate":
        mid = _sigmoid(mid)
    z = jnp.dot(mid.astype(BF16), w2b_ref[...], preferred_element_type=F32)
    if mode == "decay":
        u = -(b_ref[...] + z)
        softplus = jnp.maximum(u, 0.0) + jnp.log1p(jnp.exp(-jnp.abs(u)))
        z = -jnp.exp(-softplus - 0.5)
    elif mode == "iclr":
        z = _sigmoid(b_ref[...] + z)
    o_ref[...] = z


def lora(xs, idx, w1, w2, mode, bias=None, tm=512):
    n, m, d = xs.shape
    r = -(-w1.shape[1] // LANES) * LANES
    w1 = jnp.pad(w1, ((0, 0), (0, r - w1.shape[1])))
    w2 = jnp.pad(w2, ((0, r - w2.shape[0]), (0, 0)))
    assert (bias is None) == (mode == "gate")
    operands = [xs, w1, w2] + ([] if bias is None else [bias.reshape(1, d)])
    in_specs = [pl.BlockSpec((None, tm, d), lambda i: (idx, i, 0)),
                pl.BlockSpec((d, r), lambda i: (0, 0)),
                pl.BlockSpec((r, d), lambda i: (0, 0)),
                pl.BlockSpec((1, d), lambda i: (0, 0))]
    return pl.pallas_call(
        functools.partial(_lora_body, mode=mode),
        out_shape=jax.ShapeDtypeStruct((m, d), F32),
        grid=(m // tm,),
        in_specs=in_specs[:len(operands)],
        out_specs=pl.BlockSpec((tm, d), lambda i: (i, 0)),
        scratch_shapes=[pltpu.VMEM((d, r), BF16), pltpu.VMEM((r, d), BF16)],
        compiler_params=_params(("arbitrary",)),
        name="lora_" + mode,
    )(*operands)


def _attn_body(slopes_ref, *refs):
    qkv_refs = refs[:9]
    o_ref = refs[9]
    acc_ref, m_ref, l_ref = refs[10:]
    head = pl.program_id(1)
    blk = ATTN_BLOCK
    scale = ATTN_HEAD_DIM ** -0.5

    qi = lax.broadcasted_iota(jnp.int32, (blk, 2 * blk), 0)
    kj = lax.broadcasted_iota(jnp.int32, (blk, 2 * blk), 1)
    dist = qi + blk - kj
    distf = dist.astype(F32)

    for g, (window, dil) in enumerate(ATTN_GROUPS):
        q_ref, k_ref, v_ref = qkv_refs[3 * g:3 * g + 3]
        seq = q_ref.shape[0]
        nb = seq // dil // blk
        n_back = window // dil
        coef = slopes_ref[g, head] * float(dil)
        bias2 = jnp.where((dist >= 0) & (dist <= n_back), -(coef * distf), NEG_INF)
        bias1 = bias2[:, blk:]
        for j in range(seq // blk):
            r, n = j // nb, j % nb
            q = q_ref[j * blk:(j + 1) * blk, :]
            lo = j * blk if n == 0 else (j - 1) * blk
            k = k_ref[lo:(j + 1) * blk, :]
            v = v_ref[lo:(j + 1) * blk, :]
            s = lax.dot_general(q, k, (((1,), (1,)), ((), ())), preferred_element_type=F32)
            s = s * scale + (bias1 if n == 0 else bias2)
            m = jnp.max(s, axis=1, keepdims=True)
            p = jnp.exp(s - m)
            l = jnp.sum(p, axis=1, keepdims=True)
            acc = jnp.dot(p.astype(BF16), v, preferred_element_type=F32)
            start = n * blk * dil + r
            rows = pl.ds(start, blk) if dil == 1 else pl.ds(start, blk, stride=dil)
            acc_ref[g, rows, :] = acc
            m_ref[g, rows, :] = jnp.broadcast_to(m, acc.shape)
            l_ref[g, rows, :] = jnp.broadcast_to(l, acc.shape)

    rows_per_step = 256
    for c in range(o_ref.shape[0] // rows_per_step):
        sl = pl.ds(c * rows_per_step, rows_per_step)
        ms = [m_ref[g, sl, :] for g in range(3)]
        top = jnp.maximum(jnp.maximum(ms[0], ms[1]), ms[2])
        es = [jnp.exp(mg - top) for mg in ms]
        num = es[0] * acc_ref[0, sl, :] + es[1] * acc_ref[1, sl, :] + es[2] * acc_ref[2, sl, :]
        den = es[0] * l_ref[0, sl, :] + es[1] * l_ref[1, sl, :] + es[2] * l_ref[2, sl, :]
        o_ref[sl, :] = (num / den).astype(o_ref.dtype)


def dilated_attention(qkv, slopes, batch, seq):
    n_g = len(qkv)
    m = qkv[0].shape[0]
    hd = ATTN_HEAD_DIM
    in_specs = [pl.BlockSpec(memory_space=pltpu.SMEM)]
    for g in range(n_g):
        for c in range(3):
            in_specs.append(pl.BlockSpec((seq, hd), lambda b, h, c=c: (b, c * ATTN_HEADS + h)))
    return pl.pallas_call(
        _attn_body,
        out_shape=jax.ShapeDtypeStruct((m, ATTN_HEADS * hd), BF16),
        grid=(batch, ATTN_HEADS),
        in_specs=in_specs,
        out_specs=pl.BlockSpec((seq, hd), lambda b, h: (b, h)),
        scratch_shapes=[pltpu.VMEM((n_g, seq, hd), F32)] * 3,
        compiler_params=_params(("arbitrary", "arbitrary")),
        name="dilated_attention",
    )(slopes, *[a for a in qkv for _ in range(3)])


def _split3(x):
    hi = x.astype(BF16)
    r1 = x - hi.astype(F32)
    mid = r1.astype(BF16)
    lo = (r1 - mid.astype(F32)).astype(BF16)
    return hi, mid, lo


def _dot_exact_rhs(x, rhs_bf16):
    return sum(jnp.dot(part, rhs_bf16, preferred_element_type=F32) for part in _split3(x))


def _dot_exact_lhs(lhs_bf16, x):
    return sum(jnp.dot(lhs_bf16, part, preferred_element_type=F32) for part in _split3(x))


def _dot_t(a, b):
    return lax.dot_general(a, b, (((1,), (1,)), ((), ())), preferred_element_type=F32)


def _t_dot(a, b):
    return lax.dot_general(a, b, (((0,), (0,)), ((), ())), preferred_element_type=F32)


def _scan_chunk_body(r_ref, k_ref, v_ref, a_ref, lw_ref, g_ref, kk_ref, ka_ref, rk_ref, lng_ref, lnb_ref,
                     o_ref, state_ref, lhs_ref, rhs_ref, vs_ref, tail_ref):
    t = r_ref.shape[0]
    n_pairs = r_ref.shape[1] // LANES
    pairs = range(n_pairs)

    @pl.when(pl.program_id(1) == 0)
    def _():
        state_ref[...] = jnp.zeros_like(state_ref)

    lane = lax.broadcasted_iota(jnp.int32, (t, LANES), 1)
    head0 = lane < RWKV_HEAD_SIZE
    ri = lax.broadcasted_iota(jnp.int32, (2 * t, 2 * t), 0)
    ci = lax.broadcasted_iota(jnp.int32, (2 * t, 2 * t), 1)
    same_head = (ri < t) == (ci < t)
    strict = same_head & (ci < ri)
    incl = same_head & (ci <= ri)
    eye = jnp.where(ri == ci, 1.0, 0.0)
    tri = jnp.where(lax.broadcasted_iota(jnp.int32, (t, t), 1) <= lax.broadcasted_iota(jnp.int32, (t, t), 0),
                    1.0, 0.0).astype(BF16)
    li = lax.broadcasted_iota(jnp.int32, (LANES, LANES), 0)
    lj = lax.broadcasted_iota(jnp.int32, (LANES, LANES), 1)
    head_ones = jnp.where((li < RWKV_HEAD_SIZE) == (lj < RWKV_HEAD_SIZE), 1.0, 0.0).astype(BF16)
    n_doublings = (t - 1).bit_length()

    def lanes(x, p):
        return x[:, p * LANES:(p + 1) * LANES]

    def stack(x):
        return jnp.concatenate([jnp.where(head0, x, 0.0), jnp.where(head0, 0.0, x)], axis=0)

    def head_sum(x):
        tall = jnp.concatenate([lanes(x, p) for p in pairs], axis=0)
        s = _dot_exact_rhs(tall, head_ones)
        return jnp.concatenate([s[p * t:(p + 1) * t, :] for p in pairs], axis=1)

    def mm(a, b):
        return jnp.dot(a, b, preferred_element_type=F32)

    r = r_ref[...]
    k0 = k_ref[...]
    a_gate = a_ref[...]
    lw = lw_ref[...]
    kk = k0 * kk_ref[...]
    kk = kk / jnp.maximum(jnp.sqrt(head_sum(kk * kk)), 1e-12)
    k = k0 * (1.0 + (a_gate - 1.0) * ka_ref[...])
    b = kk * a_gate
    lg = _dot_exact_lhs(tri, lw)
    g_incl = jnp.exp(lg)
    g_inv = jnp.exp(-lg)
    g_last = g_incl[t - 1:t, :]
    g_tail = g_last * g_inv
    at = -kk * jnp.exp(lg - lw)
    rt = r * g_incl
    bt = b * g_inv
    kt = k * g_inv
    bh = b * g_tail
    kh = k * g_tail
    for p in pairs:
        lhs_ref[p, :2 * t, :] = stack(lanes(at, p)).astype(BF16)
        lhs_ref[p, 2 * t:, :] = stack(lanes(rt, p)).astype(BF16)
        btp = lanes(bt, p).astype(BF16)
        ktp = lanes(kt, p).astype(BF16)
        rhs_ref[p] = jnp.concatenate([btp, btp, ktp, ktp], axis=0)
        vs_ref[p] = stack(lanes(v_ref[...], p)).astype(BF16)
        tail_ref[p, :2 * t, :] = stack(lanes(bh, p)).astype(BF16)
        tail_ref[p, 2 * t:, :] = stack(lanes(kh, p)).astype(BF16)

    n_ab, n_ak, b_r = [], [], []
    for p in pairs:
        sc = _dot_t(lhs_ref[p], rhs_ref[p])
        n_ab.append(jnp.where(strict, sc[:2 * t, :2 * t], 0.0))
        n_ak.append(jnp.where(strict, sc[:2 * t, 2 * t:], 0.0).astype(BF16))
        b_r.append(jnp.concatenate([jnp.where(incl, sc[2 * t:, :2 * t], 0.0),
                                    jnp.where(incl, sc[2 * t:, 2 * t:], 0.0)], axis=1).astype(BF16))

    inv = [eye + n for n in n_ab]
    pw = n_ab
    for _ in range(n_doublings - 1):
        pwb = [x.astype(BF16) for x in pw]
        pw = [mm(x, x) for x in pwb]
        inv = [i_p + mm(q.astype(BF16), i_p.astype(BF16)) for i_p, q in zip(inv, pw)]
    inv = [x.astype(BF16) for x in inv]

    state = [state_ref[p] for p in pairs]
    state_b = [s.astype(BF16) for s in state]
    w = [_dot_t(lhs_ref[p, :2 * t, :], state_b[p]) + mm(n_ak[p], vs_ref[p]) for p in pairs]
    zv = [jnp.concatenate([mm(inv[p], w[p].astype(BF16)).astype(BF16), vs_ref[p]], axis=0) for p in pairs]
    y_parts = []
    for p in pairs:
        y_s = _dot_t(lhs_ref[p, 2 * t:, :], state_b[p]) + mm(b_r[p], zv[p])
        y_parts.append(y_s[:t, :] + y_s[t:, :])
        state_ref[p] = state[p] * lanes(g_last, p) + _t_dot(zv[p], tail_ref[p])
    y = jnp.concatenate(y_parts, axis=1)

    inv_n = 1.0 / RWKV_HEAD_SIZE
    mean = head_sum(y) * inv_n
    yc = y - mean
    var = head_sum(yc * yc) * inv_n
    yn = yc * lax.rsqrt(var + RWKV_GN_EPS) * lng_ref[...] + lnb_ref[...]
    bonus = head_sum(r * k * rk_ref[...]) * v_ref[...]
    o_ref[...] = ((yn + bonus) * g_ref[...]).astype(o_ref.dtype)


def rwkv_scan(rkv, a_gate, log_w, gate, k_k, k_a, r_k, ln_g, ln_b, batch, seq):
    _, m, d = rkv.shape
    t = SCAN_CHUNK
    nc = seq // t
    n_pairs = d // LANES
    row = lambda g: pl.BlockSpec((None, t, d), lambda b, c, g=g: (g, b * nc + c, 0))
    tile = pl.BlockSpec((t, d), lambda b, c: (b * nc + c, 0))
    vec = pl.BlockSpec((1, d), lambda b, c: (0, 0))
    return pl.pallas_call(
        _scan_chunk_body,
        out_shape=jax.ShapeDtypeStruct((m, d), BF16),
        grid=(batch, nc),
        in_specs=[row(0), row(1), row(2), tile, tile, tile, vec, vec, vec, vec, vec],
        out_specs=tile,
        scratch_shapes=[pltpu.VMEM((n_pairs, LANES, LANES), F32),
                        pltpu.VMEM((n_pairs, 4 * t, LANES), BF16),
                        pltpu.VMEM((n_pairs, 4 * t, LANES), BF16),
                        pltpu.VMEM((n_pairs, 2 * t, LANES), BF16),
                        pltpu.VMEM((n_pairs, 4 * t, LANES), BF16)],
        compiler_params=_params(("arbitrary", "arbitrary")),
        name="rwkv_scan",
    )(rkv, rkv, rkv, a_gate, log_w, gate,
      k_k.reshape(1, d), k_a.reshape(1, d), r_k.reshape(1, d), ln_g.reshape(1, d), ln_b.reshape(1, d))


def _alibi_slopes(n_groups, n_heads):
    n = n_groups * n_heads
    idx = jnp.arange(1, n + 1, dtype=F32)
    return (2.0 ** (-8.0 * idx / n)).reshape(n_groups, n_heads)


def kernel(x, p, attn_norm, attn_w_qkv, attn_w_o, rwkv_norm, rwkv_mu, rwkv_w_rkv, rwkv_w0, rwkv_w_w1, rwkv_w_w2, rwkv_a0, rwkv_w_a1, rwkv_w_a2, rwkv_w_g1, rwkv_w_g2, rwkv_k_k, rwkv_k_a, rwkv_r_k, rwkv_ln_g, rwkv_ln_b, rwkv_w_o, conv_norm, conv_w_in, conv_w, conv_w_out, ffn_norm, ffn_w_gu, ffn_conv_w, ffn_conv_b, ffn_w_down, ple_w_proj, ple_norm, ple_w_gate, final_norm):
    batch, seq, d = x.shape
    depth = p.shape[0]
    m = batch * seq
    n_groups = len(ATTN_GROUPS)
    slopes = _alibi_slopes(n_groups, ATTN_HEADS)
    x = x.reshape(m, d)
    p = p.reshape(depth, m, p.shape[-1])

    for i in range(depth):
        kind, j = i % 3, i // 3
        if kind == 0:
            hs = attn_rmsnorm(x, attn_norm[j], batch, seq)
            n_qkv = 3 * ATTN_HEADS * ATTN_HEAD_DIM
            qkv = [linear(hs[g], attn_w_qkv, j, BF16, n_out=n_qkv, first_col=g * n_qkv, name="attn_qkv")
                   for g in range(n_groups)]
            o = dilated_attention(qkv, slopes, batch, seq)
            x = linear(o, attn_w_o, j, F32, residual=x, name="attn_out")
        elif kind == 1:
            xs = rwkv_prep(x, rwkv_norm[j], rwkv_mu[j], seq)
            rkv = grouped_linear(xs, rwkv_w_rkv, j, F32, name="rwkv_rkv")
            log_w = lora(xs, 3, rwkv_w_w1[j], rwkv_w_w2[j], "decay", rwkv_w0[j])
            a_gate = lora(xs, 4, rwkv_w_a1[j], rwkv_w_a2[j], "iclr", rwkv_a0[j])
            gate = lora(xs, 5, rwkv_w_g1[j], rwkv_w_g2[j], "gate")
            y = rwkv_scan(rkv, a_gate, log_w, gate, rwkv_k_k[j], rwkv_k_a[j], rwkv_r_k[j].reshape(d),
                          rwkv_ln_g[j], rwkv_ln_b[j], batch, seq)
            x = linear(y, rwkv_w_o, j, F32, residual=x, name="rwkv_out")
        else:
            h = rmsnorm(x, conv_norm[j], BF16)
            y = shortconv_in(h, conv_w_in, j, conv_w[j], seq)
            x = linear(y, conv_w_out, j, F32, residual=x, name="conv_out")
        h = rmsnorm(x, ffn_norm[i], BF16)
        act = ffn_gate_up(h, ffn_w_gu, i, ffn_conv_w[i], ffn_conv_b[i], seq)
        x = linear(act, ffn_w_down, i, F32, tm=512, tn=512, residual=x, name="ffn_down")
        hn = rmsnorm(x, ple_norm[i], BF16)
        x = ple_update(x, hn, p, i, ple_w_gate, ple_w_proj)
    return rmsnorm(x, final_norm, F32).reshape(batch, seq, d)
```

```python
import functools

import jax
import jax.numpy as jnp
from jax import lax
from jax.experimental import pallas as pl
from jax.experimental.pallas import tpu as pltpu

F32 = jnp.float32
BF16 = jnp.bfloat16

NORM_EPS = 1e-6
NEG_INF = -1e30
RWKV_GN_EPS = 6.4e-4

ATTN_GROUPS = ((128, 1), (512, 4), (2048, 16))
ATTN_HEADS = 16
ATTN_HEAD_DIM = 128
ATTN_BLOCK = 128
RWKV_HEAD_SIZE = 64
SCAN_CHUNK = 64
LANES = 128
CARRY_ROWS = 8
SUB_ROWS = 256
SUB_COLS = 512

VMEM_LIMIT_BYTES = 56 * 1024 * 1024


def _params(semantics):
    return pltpu.CompilerParams(dimension_semantics=semantics, vmem_limit_bytes=VMEM_LIMIT_BYTES)


def _sigmoid(x):
    return 1.0 / (1.0 + jnp.exp(-x))


def _rms(x, g):
    ms = jnp.mean(x * x, axis=-1, keepdims=True)
    return x * lax.rsqrt(ms + NORM_EPS) * g


def _shift_rows(cur, prev_tail, n):
    row = lax.broadcasted_iota(jnp.int32, cur.shape, 0)
    out = pltpu.roll(cur, n, axis=0)
    for r in range(n):
        src = CARRY_ROWS - n + r
        out = jnp.where(row == r, prev_tail[src:src + 1, :], out)
    return out


def _attn_norm_body(x_ref, g_ref, o0_ref, o1_ref, o2_ref, y_ref, *, tm):
    y = _rms(x_ref[...], g_ref[...])
    o0_ref[...] = y.astype(o0_ref.dtype)
    for c in range(y_ref.shape[0]):
        y_ref[c] = y[:, c * LANES:(c + 1) * LANES]
    for o_ref, dil in ((o1_ref, ATTN_GROUPS[1][1]), (o2_ref, ATTN_GROUPS[2][1])):
        for r in range(dil):
            for c in range(y_ref.shape[0]):
                o_ref[r, :, c * LANES:(c + 1) * LANES] = (
                    y_ref[c, pl.ds(r, tm // dil, stride=dil), :].astype(o_ref.dtype))


def attn_rmsnorm(x, g, batch, seq, tm=256):
    m, d = x.shape
    bps = seq // tm
    d1, d2 = ATTN_GROUPS[1][1], ATTN_GROUPS[2][1]
    outs = pl.pallas_call(
        functools.partial(_attn_norm_body, tm=tm),
        out_shape=(jax.ShapeDtypeStruct((m, d), BF16),
                   jax.ShapeDtypeStruct((batch, d1, seq // d1, d), BF16),
                   jax.ShapeDtypeStruct((batch, d2, seq // d2, d), BF16)),
        grid=(batch, bps),
        in_specs=[pl.BlockSpec((tm, d), lambda b, i: (b * bps + i, 0)),
                  pl.BlockSpec((1, d), lambda b, i: (0, 0))],
        out_specs=(pl.BlockSpec((tm, d), lambda b, i: (b * bps + i, 0)),
                   pl.BlockSpec((None, d1, tm // d1, d), lambda b, i: (b, 0, i, 0)),
                   pl.BlockSpec((None, d2, tm // d2, d), lambda b, i: (b, 0, i, 0))),
        scratch_shapes=[pltpu.VMEM((d // LANES, tm, LANES), F32)],
        compiler_params=_params(("arbitrary", "arbitrary")),
        name="attn_norm",
    )(x, g.reshape(1, d))
    return [o.reshape(m, d) for o in outs]


def _rwkv_prep_body(x_ref, g_ref, mu_ref, o_ref, carry_ref, *, bps):
    first = pl.program_id(0) % bps == 0

    @pl.when(first)
    def _():
        carry_ref[...] = jnp.zeros_like(carry_ref)

    h = _rms(x_ref[...], g_ref[...])
    xx = _shift_rows(h, carry_ref[...], 1) - h
    carry_ref[...] = h[h.shape[0] - CARRY_ROWS:, :]
    for n in range(o_ref.shape[0]):
        o_ref[n] = (h + xx * mu_ref[n:n + 1, :]).astype(o_ref.dtype)


def rwkv_prep(x, g, mu, seq, tm=256):
    m, d = x.shape
    n = mu.shape[0]
    return pl.pallas_call(
        functools.partial(_rwkv_prep_body, bps=seq // tm),
        out_shape=jax.ShapeDtypeStruct((n, m, d), BF16),
        grid=(m // tm,),
        in_specs=[pl.BlockSpec((tm, d), lambda i: (i, 0)),
                  pl.BlockSpec((1, d), lambda i: (0, 0)),
                  pl.BlockSpec((n, d), lambda i: (0, 0))],
        out_specs=pl.BlockSpec((n, tm, d), lambda i: (0, i, 0)),
        scratch_shapes=[pltpu.VMEM((CARRY_ROWS, d), F32)],
        compiler_params=_params(("arbitrary",)),
        name="rwkv_prep",
    )(x, g.reshape(1, d), mu)


def _cast_weight(w_ref, wb_ref, rows=256):
    k = w_ref.shape[0]
    rows = min(rows, k)

    def chunk(c, carry):
        sl = pl.ds(pl.multiple_of(c * rows, rows), rows)
        wb_ref[sl, :] = w_ref[sl, :].astype(wb_ref.dtype)
        return carry

    lax.fori_loop(0, k // rows, chunk, 0)


def _mm_body(*refs, n_x, w_lhs, n_extra, n_scratch, epilogue):
    n_w = len(w_lhs)
    x_refs = refs[:n_x]
    w_refs = refs[n_x:n_x + n_w]
    e_refs = refs[n_x + n_w:n_x + n_w + n_extra]
    o_ref = refs[n_x + n_w + n_extra]
    wb_refs = refs[n_x + n_w + n_extra + 1:n_x + 2 * n_w + n_extra + 1]
    s_refs = refs[n_x + 2 * n_w + n_extra + 1:]
    assert len(s_refs) == n_scratch

    @pl.when(pl.program_id(2) == 0)
    def _():
        for w_ref, wb_ref in zip(w_refs, wb_refs):
            _cast_weight(w_ref, wb_ref)

    tm = o_ref.shape[0]
    sub = min(SUB_ROWS, tm)
    for c in range(tm // sub):
        rows = pl.ds(c * sub, sub)
        xs = [x_ref[rows, :].astype(BF16) for x_ref in x_refs]
        accs = [jnp.dot(xs[l], wb_ref[...], preferred_element_type=F32)
                for wb_ref, l in zip(wb_refs, w_lhs)]
        o_ref[rows, :] = epilogue(accs, e_refs, s_refs, rows, c == 0).astype(o_ref.dtype)


def matmul(xs, ws, extras, epilogue, out_shape, out_spec, grid, scratch=(), name="matmul"):
    w_lhs = tuple(l for _, _, l in ws)
    wb_scratch = [pltpu.VMEM(spec.block_shape[-2:], BF16) for _, spec, _ in ws]
    body = functools.partial(_mm_body, n_x=len(xs), w_lhs=w_lhs, n_extra=len(extras),
                             n_scratch=len(scratch), epilogue=epilogue)
    return pl.pallas_call(
        body,
        out_shape=out_shape,
        grid=grid,
        in_specs=[s for _, s in xs] + [s for _, s, _ in ws] + [s for _, s in extras],
        out_specs=out_spec,
        scratch_shapes=wb_scratch + list(scratch),
        compiler_params=_params(("arbitrary", "arbitrary", "arbitrary")),
        name=name,
    )(*[a for a, _ in xs], *[a for a, _, _ in ws], *[a for a, _ in extras])


def _ep_plain(accs, e_refs, s_refs, rows, first_sub):
    return accs[0]


def _ep_residual(accs, e_refs, s_refs, rows, first_sub):
    return e_refs[0][rows, :] + accs[0]


def _conv3(cur, cw_ref, carry_ref, bps, first_sub):
    if first_sub:
        @pl.when(pl.program_id(2) % bps == 0)
        def _():
            carry_ref[...] = jnp.zeros_like(carry_ref)

    tail = carry_ref[...]
    d1 = _shift_rows(cur, tail, 1)
    d2 = _shift_rows(cur, tail, 2)
    carry_ref[...] = cur[cur.shape[0] - CARRY_ROWS:, :]
    w = cw_ref[...]
    return d2 * w[0:1, :] + d1 * w[1:2, :] + cur * w[2:3, :]


def _ep_ffn(accs, e_refs, s_refs, rows, first_sub, *, bps):
    gate = _conv3(accs[0], e_refs[0], s_refs[0], bps, first_sub) + e_refs[1][...]
    return gate * _sigmoid(gate) * accs[1]


def _ep_shortconv(accs, e_refs, s_refs, rows, first_sub, *, bps):
    return accs[0] * _conv3(accs[1] * accs[2], e_refs[0], s_refs[0], bps, first_sub)


def _layer_cols(w, layer, tn, first_block=0):
    k = w.shape[1]
    return pl.BlockSpec((None, k, tn), lambda g, j, i: (layer, 0, first_block + j))


def linear(x, w, layer, out_dtype, tm=512, tn=1024, n_out=None, first_col=0, residual=None, name="linear"):
    m, k = x.shape
    n = n_out or w.shape[2]
    tn = min(tn, n)
    extras = []
    epilogue = _ep_plain
    if residual is not None:
        extras = [(residual, pl.BlockSpec((tm, tn), lambda g, j, i: (i, j)))]
        epilogue = _ep_residual
    return matmul(
        [(x, pl.BlockSpec((tm, k), lambda g, j, i: (i, 0)))],
        [(w, _layer_cols(w, layer, tn, first_col // tn), 0)],
        extras, epilogue,
        jax.ShapeDtypeStruct((m, n), out_dtype),
        pl.BlockSpec((tm, tn), lambda g, j, i: (i, j)),
        (1, n // tn, m // tm), name=name)


def grouped_linear(x, w, layer, out_dtype, tm=512, tn=1024, name="grouped_linear"):
    m, k = x.shape[1:]
    n_groups, _, n_out = w.shape[1:]
    tn = min(tn, n_out)
    return matmul(
        [(x, pl.BlockSpec((None, tm, k), lambda g, j, i: (g, i, 0)))],
        [(w, pl.BlockSpec((None, None, k, tn), lambda g, j, i: (layer, g, 0, j)), 0)],
        [], _ep_plain,
        jax.ShapeDtypeStruct((n_groups, m, n_out), out_dtype),
        pl.BlockSpec((None, tm, tn), lambda g, j, i: (g, i, j)),
        (n_groups, n_out // tn, m // tm), name=name)


def ffn_gate_up(h, w_gu, layer, conv_w, conv_b, seq, tm=512, tn=512):
    m, k = h.shape
    f = w_gu.shape[2] // 2
    nj = f // tn
    vec = lambda rows: pl.BlockSpec((rows, tn), lambda g, j, i: (0, j))
    return matmul(
        [(h, pl.BlockSpec((tm, k), lambda g, j, i: (i, 0)))],
        [(w_gu, _layer_cols(w_gu, layer, tn), 0),
         (w_gu, _layer_cols(w_gu, layer, tn, nj), 0)],
        [(conv_w, vec(conv_w.shape[0])), (conv_b.reshape(1, f), vec(1))],
        functools.partial(_ep_ffn, bps=seq // tm),
        jax.ShapeDtypeStruct((m, f), BF16),
        pl.BlockSpec((tm, tn), lambda g, j, i: (i, j)),
        (1, nj, m // tm),
        scratch=[pltpu.VMEM((CARRY_ROWS, tn), F32)], name="ffn_gate_up")


def shortconv_in(h, w_in, layer, conv_w, seq, tm=512, tn=512):
    m, k = h.shape
    d = w_in.shape[2] // 3
    nj = d // tn
    return matmul(
        [(h, pl.BlockSpec((tm, k), lambda g, j, i: (i, 0)))],
        [(w_in, _layer_cols(w_in, layer, tn, s * nj), 0) for s in range(3)],
        [(conv_w, pl.BlockSpec((conv_w.shape[0], tn), lambda g, j, i: (0, j)))],
        functools.partial(_ep_shortconv, bps=seq // tm),
        jax.ShapeDtypeStruct((m, d), BF16),
        pl.BlockSpec((tm, tn), lambda g, j, i: (i, j)),
        (1, nj, m // tm),
        scratch=[pltpu.VMEM((CARRY_ROWS, tn), F32)], name="shortconv_in")


def _whole_weight(w, layer):
    return pl.BlockSpec((None,) + w.shape[1:], lambda i: (layer, 0, 0), pipeline_mode=pl.Buffered(1))


def _norm_rows(o_ref, ssq, g_ref, h_ref):
    n = o_ref.shape[1]
    rstd = lax.rsqrt(ssq * (1.0 / n) + NORM_EPS)
    for c in range(n // SUB_COLS):
        cols = pl.ds(c * SUB_COLS, SUB_COLS)
        h_ref[:, cols] = (o_ref[:, cols] * rstd * g_ref[:, cols]).astype(h_ref.dtype)


def _proj_norm_body(y_ref, w_ref, res_ref, g_ref, o_ref, h_ref, wb_ref):
    @pl.when(pl.program_id(0) == 0)
    def _():
        _cast_weight(w_ref, wb_ref)

    y = y_ref[...]
    ssq = jnp.zeros((y.shape[0], 1), F32)
    for c in range(o_ref.shape[1] // SUB_COLS):
        cols = pl.ds(c * SUB_COLS, SUB_COLS)
        xn = res_ref[:, cols] + jnp.dot(y, wb_ref[:, cols], preferred_element_type=F32)
        o_ref[:, cols] = xn
        ssq = ssq + jnp.sum(xn * xn, axis=-1, keepdims=True)
    _norm_rows(o_ref, ssq, g_ref, h_ref)


def project_residual_norm(y, w, layer, residual, g, tm=512):
    m, k = y.shape
    n = w.shape[2]
    row = lambda width: pl.BlockSpec((tm, width), lambda i: (i, 0))
    return pl.pallas_call(
        _proj_norm_body,
        out_shape=(jax.ShapeDtypeStruct((m, n), F32), jax.ShapeDtypeStruct((m, n), BF16)),
        grid=(m // tm,),
        in_specs=[row(k), _whole_weight(w, layer), row(n), pl.BlockSpec((1, n), lambda i: (0, 0))],
        out_specs=(row(n), row(n)),
        scratch_shapes=[pltpu.VMEM((k, n), BF16)],
        compiler_params=_params(("arbitrary",)),
        name="project_residual_norm",
    )(y, w, residual, g.reshape(1, n))


def _ple_body(x_ref, p_ref, wg_ref, wp_ref, g_ref, *refs, norm_next):
    if norm_next:
        gn_ref, o_ref, h_ref, wgb_ref, wpb_ref = refs
    else:
        o_ref, wgb_ref, wpb_ref = refs

    @pl.when(pl.program_id(0) == 0)
    def _():
        _cast_weight(wg_ref, wgb_ref)
        _cast_weight(wp_ref, wpb_ref)

    hn = _rms(x_ref[...], g_ref[...]).astype(BF16)
    pb = p_ref[...].astype(BF16)
    ssq = jnp.zeros((hn.shape[0], 1), F32)
    for c in range(o_ref.shape[1] // SUB_COLS):
        cols = pl.ds(c * SUB_COLS, SUB_COLS)
        gate = jnp.dot(hn, wgb_ref[:, cols], preferred_element_type=F32)
        proj = jnp.dot(pb, wpb_ref[:, cols], preferred_element_type=F32)
        xn = x_ref[:, cols] + _sigmoid(gate) * proj
        o_ref[:, cols] = xn
        ssq = ssq + jnp.sum(xn * xn, axis=-1, keepdims=True)
    if norm_next:
        _norm_rows(o_ref, ssq, gn_ref, h_ref)


def ple_update(x, p, layer, w_gate, w_proj, g, g_next=None, next_dtype=BF16, tm=512):
    m, d = x.shape
    kp = p.shape[2]
    norm_next = g_next is not None
    row = pl.BlockSpec((tm, d), lambda i: (i, 0))
    vec = pl.BlockSpec((1, d), lambda i: (0, 0))
    operands = [x, p, w_gate, w_proj, g.reshape(1, d)] + ([g_next.reshape(1, d)] if norm_next else [])
    in_specs = [row, pl.BlockSpec((None, tm, kp), lambda i: (layer, i, 0)),
                _whole_weight(w_gate, layer), _whole_weight(w_proj, layer), vec, vec]
    out_shape = [jax.ShapeDtypeStruct((m, d), F32), jax.ShapeDtypeStruct((m, d), next_dtype)]
    n_out = 2 if norm_next else 1
    outs = pl.pallas_call(
        functools.partial(_ple_body, norm_next=norm_next),
        out_shape=tuple(out_shape[:n_out]),
        grid=(m // tm,),
        in_specs=in_specs[:len(operands)],
        out_specs=(row,) * n_out,
        scratch_shapes=[pltpu.VMEM((d, d), BF16), pltpu.VMEM((kp, d), BF16)],
        compiler_params=_params(("arbitrary",)),
        name="ple_update",
    )(*operands)
    return tuple(outs) if norm_next else outs[0]


def _lora_body(x_ref, w1_ref, w2_ref, *refs, mode):
    b_ref = refs[0] if mode != "gate" else None
    o_ref, w1b_ref, w2b_ref = refs[-3:]

    @pl.when(pl.program_id(0) == 0)
    def _():
        _cast_weight(w1_ref, w1b_ref)
        _cast_weight(w2_ref, w2b_ref)

    mid = jnp.dot(x_ref[...], w1b_ref[...], preferred_element_type=F32)
    if mode == "decay":
        mid = jnp.tanh(mid)
    elif mode == "gate":
        mid = _sigmoid(mid)
    z = jnp.dot(mid.astype(BF16), w2b_ref[...], preferred_element_type=F32)
    if mode == "decay":
        u = -(b_ref[...] + z)
        softplus = jnp.maximum(u, 0.0) + jnp.log1p(jnp.exp(-jnp.abs(u)))
        z = -jnp.exp(-softplus - 0.5)
    elif mode == "iclr":
        z = _sigmoid(b_ref[...] + z)
    o_ref[...] = z


def lora(xs, idx, w1, w2, mode, bias=None, tm=512):
    n, m, d = xs.shape
    r = -(-w1.shape[1] // LANES) * LANES
    w1 = jnp.pad(w1, ((0, 0), (0, r - w1.shape[1])))
    w2 = jnp.pad(w2, ((0, r - w2.shape[0]), (0, 0)))
    assert (bias is None) == (mode == "gate")
    operands = [xs, w1, w2] + ([] if bias is None else [bias.reshape(1, d)])
    in_specs = [pl.BlockSpec((None, tm, d), lambda i: (idx, i, 0)),
                pl.BlockSpec((d, r), lambda i: (0, 0)),
                pl.BlockSpec((r, d), lambda i: (0, 0)),
                pl.BlockSpec((1, d), lambda i: (0, 0))]
    return pl.pallas_call(
        functools.partial(_lora_body, mode=mode),
        out_shape=jax.ShapeDtypeStruct((m, d), F32),
        grid=(m // tm,),
        in_specs=in_specs[:len(operands)],
        out_specs=pl.BlockSpec((tm, d), lambda i: (i, 0)),
        scratch_shapes=[pltpu.VMEM((d, r), BF16), pltpu.VMEM((r, d), BF16)],
        compiler_params=_params(("arbitrary",)),
        name="lora_" + mode,
    )(*operands)


def _attn_body(slopes_ref, *refs):
    qkv_refs = refs[:9]
    o_ref = refs[9]
    acc_ref, m_ref, l_ref = refs[10:]
    head = pl.program_id(1)
    blk = ATTN_BLOCK
    scale = ATTN_HEAD_DIM ** -0.5

    qi = lax.broadcasted_iota(jnp.int32, (blk, 2 * blk), 0)
    kj = lax.broadcasted_iota(jnp.int32, (blk, 2 * blk), 1)
    dist = qi + blk - kj
    distf = dist.astype(F32)

    for g, (window, dil) in enumerate(ATTN_GROUPS):
        q_ref, k_ref, v_ref = qkv_refs[3 * g:3 * g + 3]
        seq = q_ref.shape[0]
        nb = seq // dil // blk
        n_back = window // dil
        coef = slopes_ref[g, head] * float(dil)
        bias2 = jnp.where((dist >= 0) & (dist <= n_back), -(coef * distf), NEG_INF)
        bias1 = bias2[:, blk:]
        for j in range(seq // blk):
            r, n = j // nb, j % nb
            q = q_ref[j * blk:(j + 1) * blk, :]
            lo = j * blk if n == 0 else (j - 1) * blk
            k = k_ref[lo:(j + 1) * blk, :]
            v = v_ref[lo:(j + 1) * blk, :]
            s = lax.dot_general(q, k, (((1,), (1,)), ((), ())), preferred_element_type=F32)
            s = s * scale + (bias1 if n == 0 else bias2)
            m = jnp.max(s, axis=1, keepdims=True)
            p = jnp.exp(s - m)
            l = jnp.sum(p, axis=1, keepdims=True)
            acc = jnp.dot(p.astype(BF16), v, preferred_element_type=F32)
            start = n * blk * dil + r
            rows = pl.ds(start, blk) if dil == 1 else pl.ds(start, blk, stride=dil)
            acc_ref[g, rows, :] = acc
            m_ref[g, rows, :] = jnp.broadcast_to(m, acc.shape)
            l_ref[g, rows, :] = jnp.broadcast_to(l, acc.shape)

    rows_per_step = 256
    for c in range(o_ref.shape[0] // rows_per_step):
        sl = pl.ds(c * rows_per_step, rows_per_step)
        ms = [m_ref[g, sl, :] for g in range(3)]
        top = jnp.maximum(jnp.maximum(ms[0], ms[1]), ms[2])
        es = [jnp.exp(mg - top) for mg in ms]
        num = es[0] * acc_ref[0, sl, :] + es[1] * acc_ref[1, sl, :] + es[2] * acc_ref[2, sl, :]
        den = es[0] * l_ref[0, sl, :] + es[1] * l_ref[1, sl, :] + es[2] * l_ref[2, sl, :]
        o_ref[sl, :] = (num / den).astype(o_ref.dtype)


def dilated_attention(qkv, slopes, batch, seq):
    n_g = len(qkv)
    m = qkv[0].shape[0]
    hd = ATTN_HEAD_DIM
    in_specs = [pl.BlockSpec(memory_space=pltpu.SMEM)]
    for g in range(n_g):
        for c in range(3):
            in_specs.append(pl.BlockSpec((seq, hd), lambda b, h, c=c: (b, c * ATTN_HEADS + h)))
    return pl.pallas_call(
        _attn_body,
        out_shape=jax.ShapeDtypeStruct((m, ATTN_HEADS * hd), BF16),
        grid=(batch, ATTN_HEADS),
        in_specs=in_specs,
        out_specs=pl.BlockSpec((seq, hd), lambda b, h: (b, h)),
        scratch_shapes=[pltpu.VMEM((n_g, seq, hd), F32)] * 3,
        compiler_params=_params(("arbitrary", "arbitrary")),
        name="dilated_attention",
    )(slopes, *[a for a in qkv for _ in range(3)])


def _split3(x):
    hi = x.astype(BF16)
    r1 = x - hi.astype(F32)
    mid = r1.astype(BF16)
    lo = (r1 - mid.astype(F32)).astype(BF16)
    return hi, mid, lo


def _dot_exact_rhs(x, rhs_bf16):
    return sum(jnp.dot(part, rhs_bf16, preferred_element_type=F32) for part in _split3(x))


def _dot_exact_lhs(lhs_bf16, x):
    return sum(jnp.dot(lhs_bf16, part, preferred_element_type=F32) for part in _split3(x))


def _dot_t(a, b):
    return lax.dot_general(a, b, (((1,), (1,)), ((), ())), preferred_element_type=F32)


def _t_dot(a, b):
    return lax.dot_general(a, b, (((0,), (0,)), ((), ())), preferred_element_type=F32)


def _scan_chunk_body(r_ref, k_ref, v_ref, a_ref, lw_ref, g_ref, kk_ref, ka_ref, rk_ref, lng_ref, lnb_ref,
                     o_ref, state_ref, lhs_ref, rhs_ref, vs_ref, tail_ref):
    t = r_ref.shape[0]
    n_pairs = r_ref.shape[1] // LANES
    pairs = range(n_pairs)

    @pl.when(pl.program_id(1) == 0)
    def _():
        state_ref[...] = jnp.zeros_like(state_ref)

    lane = lax.broadcasted_iota(jnp.int32, (t, LANES), 1)
    head0 = lane < RWKV_HEAD_SIZE
    ri = lax.broadcasted_iota(jnp.int32, (2 * t, 2 * t), 0)
    ci = lax.broadcasted_iota(jnp.int32, (2 * t, 2 * t), 1)
    same_head = (ri < t) == (ci < t)
    strict = same_head & (ci < ri)
    incl = same_head & (ci <= ri)
    eye = jnp.where(ri == ci, 1.0, 0.0)
    tri = jnp.where(lax.broadcasted_iota(jnp.int32, (t, t), 1) <= lax.broadcasted_iota(jnp.int32, (t, t), 0),
                    1.0, 0.0).astype(BF16)
    li = lax.broadcasted_iota(jnp.int32, (LANES, LANES), 0)
    lj = lax.broadcasted_iota(jnp.int32, (LANES, LANES), 1)
    head_ones = jnp.where((li < RWKV_HEAD_SIZE) == (lj < RWKV_HEAD_SIZE), 1.0, 0.0).astype(BF16)
    n_doublings = (t - 1).bit_length()

    def lanes(x, p):
        return x[:, p * LANES:(p + 1) * LANES]

    def stack(x):
        return jnp.concatenate([jnp.where(head0, x, 0.0), jnp.where(head0, 0.0, x)], axis=0)

    def head_sum(x):
        tall = jnp.concatenate([lanes(x, p) for p in pairs], axis=0)
        s = _dot_exact_rhs(tall, head_ones)
        return jnp.concatenate([s[p * t:(p + 1) * t, :] for p in pairs], axis=1)

    def mm(a, b):
        return jnp.dot(a, b, preferred_element_type=F32)

    r = r_ref[...]
    k0 = k_ref[...]
    a_gate = a_ref[...]
    lw = lw_ref[...]
    kk = k0 * kk_ref[...]
    kk = kk / jnp.maximum(jnp.sqrt(head_sum(kk * kk)), 1e-12)
    k = k0 * (1.0 + (a_gate - 1.0) * ka_ref[...])
    b = kk * a_gate
    lg = _dot_exact_lhs(tri, lw)
    g_incl = jnp.exp(lg)
    g_inv = jnp.exp(-lg)
    g_last = g_incl[t - 1:t, :]
    g_tail = g_last * g_inv
    at = -kk * jnp.exp(lg - lw)
    rt = r * g_incl
    bt = b * g_inv
    kt = k * g_inv
    bh = b * g_tail
    kh = k * g_tail
    for p in pairs:
        lhs_ref[p, :2 * t, :] = stack(lanes(at, p)).astype(BF16)
        lhs_ref[p, 2 * t:, :] = stack(lanes(rt, p)).astype(BF16)
        btp = lanes(bt, p).astype(BF16)
        ktp = lanes(kt, p).astype(BF16)
        rhs_ref[p] = jnp.concatenate([btp, btp, ktp, ktp], axis=0)
        vs_ref[p] = stack(lanes(v_ref[...], p)).astype(BF16)
        tail_ref[p, :2 * t, :] = stack(lanes(bh, p)).astype(BF16)
        tail_ref[p, 2 * t:, :] = stack(lanes(kh, p)).astype(BF16)

    n_ab, n_ak, b_r = [], [], []
    for p in pairs:
        sc = _dot_t(lhs_ref[p], rhs_ref[p])
        n_ab.append(jnp.where(strict, sc[:2 * t, :2 * t], 0.0))
        n_ak.append(jnp.where(strict, sc[:2 * t, 2 * t:], 0.0).astype(BF16))
        b_r.append(jnp.concatenate([jnp.where(incl, sc[2 * t:, :2 * t], 0.0),
                                    jnp.where(incl, sc[2 * t:, 2 * t:], 0.0)], axis=1).astype(BF16))

    inv = [eye + n for n in n_ab]
    pw = n_ab
    for _ in range(n_doublings - 1):
        pwb = [x.astype(BF16) for x in pw]
        pw = [mm(x, x) for x in pwb]
        inv = [i_p + mm(q.astype(BF16), i_p.astype(BF16)) for i_p, q in zip(inv, pw)]
    inv = [x.astype(BF16) for x in inv]

    state = [state_ref[p] for p in pairs]
    state_b = [s.astype(BF16) for s in state]
    w = [_dot_t(lhs_ref[p, :2 * t, :], state_b[p]) + mm(n_ak[p], vs_ref[p]) for p in pairs]
    zv = [jnp.concatenate([mm(inv[p], w[p].astype(BF16)).astype(BF16), vs_ref[p]], axis=0) for p in pairs]
    y_parts = []
    for p in pairs:
        y_s = _dot_t(lhs_ref[p, 2 * t:, :], state_b[p]) + mm(b_r[p], zv[p])
        y_parts.append(y_s[:t, :] + y_s[t:, :])
        state_ref[p] = state[p] * lanes(g_last, p) + _t_dot(zv[p], tail_ref[p])
    y = jnp.concatenate(y_parts, axis=1)

    inv_n = 1.0 / RWKV_HEAD_SIZE
    mean = head_sum(y) * inv_n
    yc = y - mean
    var = head_sum(yc * yc) * inv_n
    yn = yc * lax.rsqrt(var + RWKV_GN_EPS) * lng_ref[...] + lnb_ref[...]
    bonus = head_sum(r * k * rk_ref[...]) * v_ref[...]
    o_ref[...] = ((yn + bonus) * g_ref[...]).astype(o_ref.dtype)


def rwkv_scan(rkv, a_gate, log_w, gate, k_k, k_a, r_k, ln_g, ln_b, batch, seq):
    _, m, d = rkv.shape
    t = SCAN_CHUNK
    nc = seq // t
    n_pairs = d // LANES
    row = lambda g: pl.BlockSpec((None, t, d), lambda b, c, g=g: (g, b * nc + c, 0))
    tile = pl.BlockSpec((t, d), lambda b, c: (b * nc + c, 0))
    vec = pl.BlockSpec((1, d), lambda b, c: (0, 0))
    return pl.pallas_call(
        _scan_chunk_body,
        out_shape=jax.ShapeDtypeStruct((m, d), BF16),
        grid=(batch, nc),
        in_specs=[row(0), row(1), row(2), tile, tile, tile, vec, vec, vec, vec, vec],
        out_specs=tile,
        scratch_shapes=[pltpu.VMEM((n_pairs, LANES, LANES), F32),
                        pltpu.VMEM((n_pairs, 4 * t, LANES), BF16),
                        pltpu.VMEM((n_pairs, 4 * t, LANES), BF16),
                        pltpu.VMEM((n_pairs, 2 * t, LANES), BF16),
                        pltpu.VMEM((n_pairs, 4 * t, LANES), BF16)],
        compiler_params=_params(("arbitrary", "arbitrary")),
        name="rwkv_scan",
    )(rkv, rkv, rkv, a_gate, log_w, gate,
      k_k.reshape(1, d), k_a.reshape(1, d), r_k.reshape(1, d), ln_g.reshape(1, d), ln_b.reshape(1, d))


def _alibi_slopes(n_groups, n_heads):
    n = n_groups * n_heads
    idx = jnp.arange(1, n + 1, dtype=F32)
    return (2.0 ** (-8.0 * idx / n)).reshape(n_groups, n_heads)


def kernel(x, p, attn_norm, attn_w_qkv, attn_w_o, rwkv_norm, rwkv_mu, rwkv_w_rkv, rwkv_w0, rwkv_w_w1, rwkv_w_w2, rwkv_a0, rwkv_w_a1, rwkv_w_a2, rwkv_w_g1, rwkv_w_g2, rwkv_k_k, rwkv_k_a, rwkv_r_k, rwkv_ln_g, rwkv_ln_b, rwkv_w_o, conv_norm, conv_w_in, conv_w, conv_w_out, ffn_norm, ffn_w_gu, ffn_conv_w, ffn_conv_b, ffn_w_down, ple_w_proj, ple_norm, ple_w_gate, final_norm):
    batch, seq, d = x.shape
    depth = p.shape[0]
    m = batch * seq
    n_groups = len(ATTN_GROUPS)
    slopes = _alibi_slopes(n_groups, ATTN_HEADS)
    x = x.reshape(m, d)
    p = p.reshape(depth, m, p.shape[-1])

    h_next = None
    for i in range(depth):
        kind, j = i % 3, i // 3
        if kind == 0:
            hs = attn_rmsnorm(x, attn_norm[j], batch, seq)
            n_qkv = 3 * ATTN_HEADS * ATTN_HEAD_DIM
            qkv = [linear(hs[g], attn_w_qkv, j, BF16, tm=1024, n_out=n_qkv, first_col=g * n_qkv, name="attn_qkv")
                   for g in range(n_groups)]
            y = dilated_attention(qkv, slopes, batch, seq)
            w_out = attn_w_o
        elif kind == 1:
            xs = rwkv_prep(x, rwkv_norm[j], rwkv_mu[j], seq)
            rkv = grouped_linear(xs, rwkv_w_rkv, j, F32, tm=1024, name="rwkv_rkv")
            log_w = lora(xs, 3, rwkv_w_w1[j], rwkv_w_w2[j], "decay", rwkv_w0[j])
            a_gate = lora(xs, 4, rwkv_w_a1[j], rwkv_w_a2[j], "iclr", rwkv_a0[j])
            gate = lora(xs, 5, rwkv_w_g1[j], rwkv_w_g2[j], "gate")
            y = rwkv_scan(rkv, a_gate, log_w, gate, rwkv_k_k[j], rwkv_k_a[j], rwkv_r_k[j].reshape(d),
                          rwkv_ln_g[j], rwkv_ln_b[j], batch, seq)
            w_out = rwkv_w_o
        else:
            y = shortconv_in(h_next, conv_w_in, j, conv_w[j], seq, tm=1024)
            w_out = conv_w_out
        x, h = project_residual_norm(y, w_out, j, x, ffn_norm[i])
        act = ffn_gate_up(h, ffn_w_gu, i, ffn_conv_w[i], ffn_conv_b[i], seq, tm=1024)
        x = linear(act, ffn_w_down, i, F32, tm=512, tn=512, residual=x, name="ffn_down")
        if i + 1 == depth:
            _, out = ple_update(x, p, i, ple_w_gate, ple_w_proj, ple_norm[i], final_norm, F32)
            return out.reshape(batch, seq, d)
        if (i + 1) % 3 == 2:
            x, h_next = ple_update(x, p, i, ple_w_gate, ple_w_proj, ple_norm[i], conv_norm[(i + 1) // 3])
        else:
            x = ple_update(x, p, i, ple_w_gate, ple_w_proj, ple_norm[i])
```

```python
import functools

import jax
import jax.numpy as jnp
from jax import lax
from jax.experimental import pallas as pl
from jax.experimental.pallas import tpu as pltpu

F32 = jnp.float32
BF16 = jnp.bfloat16

NORM_EPS = 1e-6
NEG_INF = -1e30
RWKV_GN_EPS = 6.4e-4

ATTN_GROUPS = ((128, 1), (512, 4), (2048, 16))
ATTN_HEADS = 16
ATTN_HEAD_DIM = 128
ATTN_BLOCK = 128
ATTN_BLOCKS_IN_FLIGHT = 16
LOG2_E = 1.4426950408889634
RWKV_HEAD_SIZE = 64
SCAN_CHUNK = 64
LANES = 128
CARRY_ROWS = 8
SUB_ROWS = 256
SUB_COLS = 512

VMEM_LIMIT_BYTES = 56 * 1024 * 1024


def _params(semantics):
    return pltpu.CompilerParams(dimension_semantics=semantics, vmem_limit_bytes=VMEM_LIMIT_BYTES)


def _sigmoid(x):
    return 1.0 / (1.0 + jnp.exp(-x))


def _rms(x, g):
    ms = jnp.mean(x * x, axis=-1, keepdims=True)
    return x * lax.rsqrt(ms + NORM_EPS) * g


def _shift_rows(cur, prev_tail, n):
    row = lax.broadcasted_iota(jnp.int32, cur.shape, 0)
    out = pltpu.roll(cur, n, axis=0)
    for r in range(n):
        src = CARRY_ROWS - n + r
        out = jnp.where(row == r, prev_tail[src:src + 1, :], out)
    return out


def _attn_norm_body(x_ref, g_ref, o0_ref, o1_ref, o2_ref, y_ref, *, tm):
    y = _rms(x_ref[...], g_ref[...])
    o0_ref[...] = y.astype(o0_ref.dtype)
    for c in range(y_ref.shape[0]):
        y_ref[c] = y[:, c * LANES:(c + 1) * LANES]
    for o_ref, dil in ((o1_ref, ATTN_GROUPS[1][1]), (o2_ref, ATTN_GROUPS[2][1])):
        for r in range(dil):
            for c in range(y_ref.shape[0]):
                o_ref[r, :, c * LANES:(c + 1) * LANES] = (
                    y_ref[c, pl.ds(r, tm // dil, stride=dil), :].astype(o_ref.dtype))


def attn_rmsnorm(x, g, batch, seq, tm=256):
    m, d = x.shape
    bps = seq // tm
    d1, d2 = ATTN_GROUPS[1][1], ATTN_GROUPS[2][1]
    outs = pl.pallas_call(
        functools.partial(_attn_norm_body, tm=tm),
        out_shape=(jax.ShapeDtypeStruct((m, d), BF16),
                   jax.ShapeDtypeStruct((batch, d1, seq // d1, d), BF16),
                   jax.ShapeDtypeStruct((batch, d2, seq // d2, d), BF16)),
        grid=(batch, bps),
        in_specs=[pl.BlockSpec((tm, d), lambda b, i: (b * bps + i, 0)),
                  pl.BlockSpec((1, d), lambda b, i: (0, 0))],
        out_specs=(pl.BlockSpec((tm, d), lambda b, i: (b * bps + i, 0)),
                   pl.BlockSpec((None, d1, tm // d1, d), lambda b, i: (b, 0, i, 0)),
                   pl.BlockSpec((None, d2, tm // d2, d), lambda b, i: (b, 0, i, 0))),
        scratch_shapes=[pltpu.VMEM((d // LANES, tm, LANES), F32)],
        compiler_params=_params(("arbitrary", "arbitrary")),
        name="attn_norm",
    )(x, g.reshape(1, d))
    return [o.reshape(m, d) for o in outs]


def _rwkv_prep_body(x_ref, g_ref, mu_ref, o_ref, carry_ref, *, bps):
    first = pl.program_id(0) % bps == 0

    @pl.when(first)
    def _():
        carry_ref[...] = jnp.zeros_like(carry_ref)

    h = _rms(x_ref[...], g_ref[...])
    xx = _shift_rows(h, carry_ref[...], 1) - h
    carry_ref[...] = h[h.shape[0] - CARRY_ROWS:, :]
    for n in range(o_ref.shape[0]):
        o_ref[n] = (h + xx * mu_ref[n:n + 1, :]).astype(o_ref.dtype)


def rwkv_prep(x, g, mu, seq, tm=256):
    m, d = x.shape
    n = mu.shape[0]
    return pl.pallas_call(
        functools.partial(_rwkv_prep_body, bps=seq // tm),
        out_shape=jax.ShapeDtypeStruct((n, m, d), BF16),
        grid=(m // tm,),
        in_specs=[pl.BlockSpec((tm, d), lambda i: (i, 0)),
                  pl.BlockSpec((1, d), lambda i: (0, 0)),
                  pl.BlockSpec((n, d), lambda i: (0, 0))],
        out_specs=pl.BlockSpec((n, tm, d), lambda i: (0, i, 0)),
        scratch_shapes=[pltpu.VMEM((CARRY_ROWS, d), F32)],
        compiler_params=_params(("arbitrary",)),
        name="rwkv_prep",
    )(x, g.reshape(1, d), mu)


def _cast_weight(w_ref, wb_ref, rows=256):
    k = w_ref.shape[0]
    rows = min(rows, k)

    def chunk(c, carry):
        sl = pl.ds(pl.multiple_of(c * rows, rows), rows)
        wb_ref[sl, :] = w_ref[sl, :].astype(wb_ref.dtype)
        return carry

    lax.fori_loop(0, k // rows, chunk, 0)


def _mm_body(*refs, n_x, w_lhs, n_extra, n_scratch, epilogue):
    n_w = len(w_lhs)
    x_refs = refs[:n_x]
    w_refs = refs[n_x:n_x + n_w]
    e_refs = refs[n_x + n_w:n_x + n_w + n_extra]
    o_ref = refs[n_x + n_w + n_extra]
    wb_refs = refs[n_x + n_w + n_extra + 1:n_x + 2 * n_w + n_extra + 1]
    s_refs = refs[n_x + 2 * n_w + n_extra + 1:]
    assert len(s_refs) == n_scratch

    @pl.when(pl.program_id(2) == 0)
    def _():
        for w_ref, wb_ref in zip(w_refs, wb_refs):
            _cast_weight(w_ref, wb_ref)

    tm = o_ref.shape[0]
    sub = min(SUB_ROWS, tm)
    for c in range(tm // sub):
        rows = pl.ds(c * sub, sub)
        xs = [x_ref[rows, :].astype(BF16) for x_ref in x_refs]
        accs = [jnp.dot(xs[l], wb_ref[...], preferred_element_type=F32)
                for wb_ref, l in zip(wb_refs, w_lhs)]
        o_ref[rows, :] = epilogue(accs, e_refs, s_refs, rows, c == 0).astype(o_ref.dtype)


def matmul(xs, ws, extras, epilogue, out_shape, out_spec, grid, scratch=(), name="matmul"):
    w_lhs = tuple(l for _, _, l in ws)
    wb_scratch = [pltpu.VMEM(spec.block_shape[-2:], BF16) for _, spec, _ in ws]
    body = functools.partial(_mm_body, n_x=len(xs), w_lhs=w_lhs, n_extra=len(extras),
                             n_scratch=len(scratch), epilogue=epilogue)
    return pl.pallas_call(
        body,
        out_shape=out_shape,
        grid=grid,
        in_specs=[s for _, s in xs] + [s for _, s, _ in ws] + [s for _, s in extras],
        out_specs=out_spec,
        scratch_shapes=wb_scratch + list(scratch),
        compiler_params=_params(("arbitrary", "arbitrary", "arbitrary")),
        name=name,
    )(*[a for a, _ in xs], *[a for a, _, _ in ws], *[a for a, _ in extras])


def _ep_plain(accs, e_refs, s_refs, rows, first_sub):
    return accs[0]


def _ep_residual(accs, e_refs, s_refs, rows, first_sub):
    return e_refs[0][rows, :] + accs[0]


def _ep_col_scale(accs, e_refs, s_refs, rows, first_sub):
    return accs[0] * e_refs[0][...]


def _conv3(cur, cw_ref, carry_ref, bps, first_sub):
    if first_sub:
        @pl.when(pl.program_id(2) % bps == 0)
        def _():
            carry_ref[...] = jnp.zeros_like(carry_ref)

    tail = carry_ref[...]
    d1 = _shift_rows(cur, tail, 1)
    d2 = _shift_rows(cur, tail, 2)
    carry_ref[...] = cur[cur.shape[0] - CARRY_ROWS:, :]
    w = cw_ref[...]
    return d2 * w[0:1, :] + d1 * w[1:2, :] + cur * w[2:3, :]


def _ep_ffn(accs, e_refs, s_refs, rows, first_sub, *, bps):
    gate = _conv3(accs[0], e_refs[0], s_refs[0], bps, first_sub) + e_refs[1][...]
    return gate * _sigmoid(gate) * accs[1]


def _ep_shortconv(accs, e_refs, s_refs, rows, first_sub, *, bps):
    return accs[0] * _conv3(accs[1] * accs[2], e_refs[0], s_refs[0], bps, first_sub)


def _layer_cols(w, layer, tn, first_block=0):
    k = w.shape[1]
    return pl.BlockSpec((None, k, tn), lambda g, j, i: (layer, 0, first_block + j))


def linear(x, w, layer, out_dtype, tm=512, tn=1024, n_out=None, first_col=0, residual=None, col_scale=None,
           name="linear"):
    m, k = x.shape
    n = n_out or w.shape[2]
    tn = min(tn, n)
    extras = []
    epilogue = _ep_plain
    assert residual is None or col_scale is None
    if residual is not None:
        extras = [(residual, pl.BlockSpec((tm, tn), lambda g, j, i: (i, j)))]
        epilogue = _ep_residual
    if col_scale is not None:
        extras = [(col_scale, pl.BlockSpec((1, tn), lambda g, j, i: (0, j)))]
        epilogue = _ep_col_scale
    return matmul(
        [(x, pl.BlockSpec((tm, k), lambda g, j, i: (i, 0)))],
        [(w, _layer_cols(w, layer, tn, first_col // tn), 0)],
        extras, epilogue,
        jax.ShapeDtypeStruct((m, n), out_dtype),
        pl.BlockSpec((tm, tn), lambda g, j, i: (i, j)),
        (1, n // tn, m // tm), name=name)


def grouped_linear(x, w, layer, out_dtype, tm=512, tn=1024, name="grouped_linear"):
    m, k = x.shape[1:]
    n_groups, _, n_out = w.shape[1:]
    tn = min(tn, n_out)
    return matmul(
        [(x, pl.BlockSpec((None, tm, k), lambda g, j, i: (g, i, 0)))],
        [(w, pl.BlockSpec((None, None, k, tn), lambda g, j, i: (layer, g, 0, j)), 0)],
        [], _ep_plain,
        jax.ShapeDtypeStruct((n_groups, m, n_out), out_dtype),
        pl.BlockSpec((None, tm, tn), lambda g, j, i: (g, i, j)),
        (n_groups, n_out // tn, m // tm), name=name)


def ffn_gate_up(h, w_gu, layer, conv_w, conv_b, seq, tm=512, tn=512):
    m, k = h.shape
    f = w_gu.shape[2] // 2
    nj = f // tn
    vec = lambda rows: pl.BlockSpec((rows, tn), lambda g, j, i: (0, j))
    return matmul(
        [(h, pl.BlockSpec((tm, k), lambda g, j, i: (i, 0)))],
        [(w_gu, _layer_cols(w_gu, layer, tn), 0),
         (w_gu, _layer_cols(w_gu, layer, tn, nj), 0)],
        [(conv_w, vec(conv_w.shape[0])), (conv_b.reshape(1, f), vec(1))],
        functools.partial(_ep_ffn, bps=seq // tm),
        jax.ShapeDtypeStruct((m, f), BF16),
        pl.BlockSpec((tm, tn), lambda g, j, i: (i, j)),
        (1, nj, m // tm),
        scratch=[pltpu.VMEM((CARRY_ROWS, tn), F32)], name="ffn_gate_up")


def shortconv_in(h, w_in, layer, conv_w, seq, tm=512, tn=512):
    m, k = h.shape
    d = w_in.shape[2] // 3
    nj = d // tn
    return matmul(
        [(h, pl.BlockSpec((tm, k), lambda g, j, i: (i, 0)))],
        [(w_in, _layer_cols(w_in, layer, tn, s * nj), 0) for s in range(3)],
        [(conv_w, pl.BlockSpec((conv_w.shape[0], tn), lambda g, j, i: (0, j)))],
        functools.partial(_ep_shortconv, bps=seq // tm),
        jax.ShapeDtypeStruct((m, d), BF16),
        pl.BlockSpec((tm, tn), lambda g, j, i: (i, j)),
        (1, nj, m // tm),
        scratch=[pltpu.VMEM((CARRY_ROWS, tn), F32)], name="shortconv_in")


def _whole_weight(w, layer):
    return pl.BlockSpec((None,) + w.shape[1:], lambda i: (layer, 0, 0), pipeline_mode=pl.Buffered(1))


def _norm_rows(o_ref, ssq, g_ref, h_ref):
    n = o_ref.shape[1]
    rstd = lax.rsqrt(ssq * (1.0 / n) + NORM_EPS)
    for c in range(n // SUB_COLS):
        cols = pl.ds(c * SUB_COLS, SUB_COLS)
        h_ref[:, cols] = (o_ref[:, cols] * rstd * g_ref[:, cols]).astype(h_ref.dtype)


def _proj_norm_body(y_ref, w_ref, res_ref, g_ref, o_ref, h_ref, wb_ref):
    @pl.when(pl.program_id(0) == 0)
    def _():
        _cast_weight(w_ref, wb_ref)

    y = y_ref[...]
    ssq = jnp.zeros((y.shape[0], 1), F32)
    for c in range(o_ref.shape[1] // SUB_COLS):
        cols = pl.ds(c * SUB_COLS, SUB_COLS)
        xn = res_ref[:, cols] + jnp.dot(y, wb_ref[:, cols], preferred_element_type=F32)
        o_ref[:, cols] = xn
        ssq = ssq + jnp.sum(xn * xn, axis=-1, keepdims=True)
    _norm_rows(o_ref, ssq, g_ref, h_ref)


def project_residual_norm(y, w, layer, residual, g, tm=512):
    m, k = y.shape
    n = w.shape[2]
    row = lambda width: pl.BlockSpec((tm, width), lambda i: (i, 0))
    return pl.pallas_call(
        _proj_norm_body,
        out_shape=(jax.ShapeDtypeStruct((m, n), F32), jax.ShapeDtypeStruct((m, n), BF16)),
        grid=(m // tm,),
        in_specs=[row(k), _whole_weight(w, layer), row(n), pl.BlockSpec((1, n), lambda i: (0, 0))],
        out_specs=(row(n), row(n)),
        scratch_shapes=[pltpu.VMEM((k, n), BF16)],
        compiler_params=_params(("arbitrary",)),
        name="project_residual_norm",
    )(y, w, residual, g.reshape(1, n))


def _ple_body(x_ref, p_ref, wg_ref, wp_ref, g_ref, *refs, norm_next):
    if norm_next:
        gn_ref, o_ref, h_ref, wgb_ref, wpb_ref = refs
    else:
        o_ref, wgb_ref, wpb_ref = refs

    @pl.when(pl.program_id(0) == 0)
    def _():
        _cast_weight(wg_ref, wgb_ref)
        _cast_weight(wp_ref, wpb_ref)

    hn = _rms(x_ref[...], g_ref[...]).astype(BF16)
    pb = p_ref[...].astype(BF16)
    ssq = jnp.zeros((hn.shape[0], 1), F32)
    for c in range(o_ref.shape[1] // SUB_COLS):
        cols = pl.ds(c * SUB_COLS, SUB_COLS)
        gate = jnp.dot(hn, wgb_ref[:, cols], preferred_element_type=F32)
        proj = jnp.dot(pb, wpb_ref[:, cols], preferred_element_type=F32)
        xn = x_ref[:, cols] + _sigmoid(gate) * proj
        o_ref[:, cols] = xn
        ssq = ssq + jnp.sum(xn * xn, axis=-1, keepdims=True)
    if norm_next:
        _norm_rows(o_ref, ssq, gn_ref, h_ref)


def ple_update(x, p, layer, w_gate, w_proj, g, g_next=None, next_dtype=BF16, tm=512):
    m, d = x.shape
    kp = p.shape[2]
    norm_next = g_next is not None
    row = pl.BlockSpec((tm, d), lambda i: (i, 0))
    vec = pl.BlockSpec((1, d), lambda i: (0, 0))
    operands = [x, p, w_gate, w_proj, g.reshape(1, d)] + ([g_next.reshape(1, d)] if norm_next else [])
    in_specs = [row, pl.BlockSpec((None, tm, kp), lambda i: (layer, i, 0)),
                _whole_weight(w_gate, layer), _whole_weight(w_proj, layer), vec, vec]
    out_shape = [jax.ShapeDtypeStruct((m, d), F32), jax.ShapeDtypeStruct((m, d), next_dtype)]
    n_out = 2 if norm_next else 1
    outs = pl.pallas_call(
        functools.partial(_ple_body, norm_next=norm_next),
        out_shape=tuple(out_shape[:n_out]),
        grid=(m // tm,),
        in_specs=in_specs[:len(operands)],
        out_specs=(row,) * n_out,
        scratch_shapes=[pltpu.VMEM((d, d), BF16), pltpu.VMEM((kp, d), BF16)],
        compiler_params=_params(("arbitrary",)),
        name="ple_update",
    )(*operands)
    return tuple(outs) if norm_next else outs[0]


def _lora_body(x_ref, w1_ref, w2_ref, *refs, mode):
    b_ref = refs[0] if mode != "gate" else None
    o_ref, w1b_ref, w2b_ref = refs[-3:]

    @pl.when(pl.program_id(0) == 0)
    def _():
        _cast_weight(w1_ref, w1b_ref)
        _cast_weight(w2_ref, w2b_ref)

    mid = jnp.dot(x_ref[...], w1b_ref[...], preferred_element_type=F32)
    if mode == "decay":
        mid = jnp.tanh(mid)
    elif mode == "gate":
        mid = _sigmoid(mid)
    z = jnp.dot(mid.astype(BF16), w2b_ref[...], preferred_element_type=F32)
    if mode == "decay":
        u = -(b_ref[...] + z)
        softplus = jnp.maximum(u, 0.0) + jnp.log1p(jnp.exp(-jnp.abs(u)))
        z = -jnp.exp(-softplus - 0.5)
    elif mode == "iclr":
        z = _sigmoid(b_ref[...] + z)
    o_ref[...] = z


def lora(xs, idx, w1, w2, mode, bias=None, tm=512):
    n, m, d = xs.shape
    r = -(-w1.shape[1] // LANES) * LANES
    w1 = jnp.pad(w1, ((0, 0), (0, r - w1.shape[1])))
    w2 = jnp.pad(w2, ((0, r - w2.shape[0]), (0, 0)))
    assert (bias is None) == (mode == "gate")
    operands = [xs, w1, w2] + ([] if bias is None else [bias.reshape(1, d)])
    in_specs = [pl.BlockSpec((None, tm, d), lambda i: (idx, i, 0)),
                pl.BlockSpec((d, r), lambda i: (0, 0)),
                pl.BlockSpec((r, d), lambda i: (0, 0)),
                pl.BlockSpec((1, d), lambda i: (0, 0))]
    return pl.pallas_call(
        functools.partial(_lora_body, mode=mode),
        out_shape=jax.ShapeDtypeStruct((m, d), F32),
        grid=(m // tm,),
        in_specs=in_specs[:len(operands)],
        out_specs=pl.BlockSpec((tm, d), lambda i: (i, 0)),
        scratch_shapes=[pltpu.VMEM((d, r), BF16), pltpu.VMEM((r, d), BF16)],
        compiler_params=_params(("arbitrary",)),
        name="lora_" + mode,
    )(*operands)


def _attn_body(slopes_ref, *refs):
    qkv_refs = refs[:9]
    o_ref = refs[9]
    acc_ref, m_ref, l_ref = refs[10:]
    head = pl.program_id(1)
    blk = ATTN_BLOCK

    qi = lax.broadcasted_iota(jnp.int32, (blk, 2 * blk), 0)
    kj = lax.broadcasted_iota(jnp.int32, (blk, 2 * blk), 1)
    dist = qi + blk - kj
    distf = dist.astype(F32)

    blocks = []
    for g, (window, dil) in enumerate(ATTN_GROUPS):
        seq = qkv_refs[3 * g].shape[0]
        nb = seq // dil // blk
        n_back = window // dil
        coef = slopes_ref[g, head] * (float(dil) * LOG2_E)
        bias2 = jnp.where((dist >= 0) & (dist <= n_back), -(coef * distf), NEG_INF)
        bias1 = bias2[:, blk:]
        for j in range(seq // blk):
            blocks.append((g, dil, j // nb, j % nb, j, bias1, bias2))

    for b0 in range(0, len(blocks), ATTN_BLOCKS_IN_FLIGHT):
        batch = blocks[b0:b0 + ATTN_BLOCKS_IN_FLIGHT]
        s_list, v_list = [], []
        for g, dil, r, n, j, bias1, bias2 in batch:
            q_ref, k_ref, v_ref = qkv_refs[3 * g:3 * g + 3]
            lo = j * blk if n == 0 else (j - 1) * blk
            s = lax.dot_general(q_ref[j * blk:(j + 1) * blk, :], k_ref[lo:(j + 1) * blk, :],
                                (((1,), (1,)), ((), ())), preferred_element_type=F32)
            s_list.append(s + (bias1 if n == 0 else bias2))
            v_list.append(v_ref[lo:(j + 1) * blk, :])
        m_list = [jnp.max(s, axis=1, keepdims=True) for s in s_list]
        p_list = [jnp.exp2(s - m) for s, m in zip(s_list, m_list)]
        l_list = [jnp.sum(p, axis=1, keepdims=True) for p in p_list]
        acc_list = [jnp.dot(p.astype(BF16), v, preferred_element_type=F32) for p, v in zip(p_list, v_list)]
        for (g, dil, r, n, j, _, _), m, l, acc in zip(batch, m_list, l_list, acc_list):
            start = n * blk * dil + r
            rows = pl.ds(start, blk) if dil == 1 else pl.ds(start, blk, stride=dil)
            acc_ref[g, rows, :] = acc
            m_ref[g, rows, :] = jnp.broadcast_to(m, acc.shape)
            l_ref[g, rows, :] = jnp.broadcast_to(l, acc.shape)

    rows_per_step = 256
    for c in range(o_ref.shape[0] // rows_per_step):
        sl = pl.ds(c * rows_per_step, rows_per_step)
        ms = [m_ref[g, sl, :] for g in range(3)]
        top = jnp.maximum(jnp.maximum(ms[0], ms[1]), ms[2])
        es = [jnp.exp2(mg - top) for mg in ms]
        num = es[0] * acc_ref[0, sl, :] + es[1] * acc_ref[1, sl, :] + es[2] * acc_ref[2, sl, :]
        den = es[0] * l_ref[0, sl, :] + es[1] * l_ref[1, sl, :] + es[2] * l_ref[2, sl, :]
        o_ref[sl, :] = (num / den).astype(o_ref.dtype)


def dilated_attention(qkv, slopes, batch, seq):
    n_g = len(qkv)
    m = qkv[0].shape[0]
    hd = ATTN_HEAD_DIM
    in_specs = [pl.BlockSpec(memory_space=pltpu.SMEM)]
    for g in range(n_g):
        for c in range(3):
            in_specs.append(pl.BlockSpec((seq, hd), lambda b, h, c=c: (b, c * ATTN_HEADS + h)))
    return pl.pallas_call(
        _attn_body,
        out_shape=jax.ShapeDtypeStruct((m, ATTN_HEADS * hd), BF16),
        grid=(batch, ATTN_HEADS),
        in_specs=in_specs,
        out_specs=pl.BlockSpec((seq, hd), lambda b, h: (b, h)),
        scratch_shapes=[pltpu.VMEM((n_g, seq, hd), F32)] * 3,
        compiler_params=_params(("arbitrary", "arbitrary")),
        name="dilated_attention",
    )(slopes, *[a for a in qkv for _ in range(3)])


def _split3(x):
    hi = x.astype(BF16)
    r1 = x - hi.astype(F32)
    mid = r1.astype(BF16)
    lo = (r1 - mid.astype(F32)).astype(BF16)
    return hi, mid, lo


def _dot_split2_rhs(x, rhs_bf16):
    hi = x.astype(BF16)
    lo = (x - hi.astype(F32)).astype(BF16)
    return jnp.dot(hi, rhs_bf16, preferred_element_type=F32) + jnp.dot(lo, rhs_bf16, preferred_element_type=F32)


def _dot_exact_lhs(lhs_bf16, x):
    return sum(jnp.dot(lhs_bf16, part, preferred_element_type=F32) for part in _split3(x))


def _dot_t(a, b):
    return lax.dot_general(a, b, (((1,), (1,)), ((), ())), preferred_element_type=F32)


def _t_dot(a, b):
    return lax.dot_general(a, b, (((0,), (0,)), ((), ())), preferred_element_type=F32)


def _scan_chunk_body(r_ref, k_ref, v_ref, a_ref, lw_ref, g_ref, kk_ref, ka_ref, rk_ref, lng_ref, lnb_ref,
                     o_ref, state_ref, lhs_ref, rhs_ref, vs_ref, tail_ref):
    t = r_ref.shape[0]
    n_pairs = r_ref.shape[1] // LANES
    pairs = range(n_pairs)

    @pl.when(pl.program_id(1) == 0)
    def _():
        state_ref[...] = jnp.zeros_like(state_ref)

    lane = lax.broadcasted_iota(jnp.int32, (t, LANES), 1)
    head0 = lane < RWKV_HEAD_SIZE
    ri = lax.broadcasted_iota(jnp.int32, (2 * t, 2 * t), 0)
    ci = lax.broadcasted_iota(jnp.int32, (2 * t, 2 * t), 1)
    same_head = (ri < t) == (ci < t)
    strict = same_head & (ci < ri)
    incl = same_head & (ci <= ri)
    eye = jnp.where(ri == ci, 1.0, 0.0)
    tri = jnp.where(lax.broadcasted_iota(jnp.int32, (t, t), 1) <= lax.broadcasted_iota(jnp.int32, (t, t), 0),
                    1.0, 0.0).astype(BF16)
    li = lax.broadcasted_iota(jnp.int32, (LANES, LANES), 0)
    lj = lax.broadcasted_iota(jnp.int32, (LANES, LANES), 1)
    head_ones = jnp.where((li < RWKV_HEAD_SIZE) == (lj < RWKV_HEAD_SIZE), 1.0, 0.0).astype(BF16)
    n_doublings = (t - 1).bit_length()

    def lanes(x, p):
        return x[:, p * LANES:(p + 1) * LANES]

    def stack(x):
        return jnp.concatenate([jnp.where(head0, x, 0.0), jnp.where(head0, 0.0, x)], axis=0)

    def head_sum(x):
        tall = jnp.concatenate([lanes(x, p) for p in pairs], axis=0)
        s = _dot_split2_rhs(tall, head_ones)
        return jnp.concatenate([s[p * t:(p + 1) * t, :] for p in pairs], axis=1)

    def mm(a, b):
        return jnp.dot(a, b, preferred_element_type=F32)

    r = r_ref[...]
    k0 = k_ref[...]
    a_gate = a_ref[...]
    lw = lw_ref[...]
    kk = k0 * kk_ref[...]
    kk = kk / jnp.maximum(jnp.sqrt(head_sum(kk * kk)), 1e-12)
    k = k0 * (1.0 + (a_gate - 1.0) * ka_ref[...])
    b = kk * a_gate
    lg = _dot_exact_lhs(tri, lw)
    g_incl = jnp.exp(lg)
    g_inv = jnp.exp(-lg)
    g_last = g_incl[t - 1:t, :]
    g_tail = g_last * g_inv
    at = -kk * jnp.exp(lg - lw)
    rt = r * g_incl
    bt = b * g_inv
    kt = k * g_inv
    bh = b * g_tail
    kh = k * g_tail
    for p in pairs:
        lhs_ref[p, :2 * t, :] = stack(lanes(at, p)).astype(BF16)
        lhs_ref[p, 2 * t:, :] = stack(lanes(rt, p)).astype(BF16)
        btp = lanes(bt, p).astype(BF16)
        ktp = lanes(kt, p).astype(BF16)
        rhs_ref[p] = jnp.concatenate([btp, btp, ktp, ktp], axis=0)
        vs_ref[p] = stack(lanes(v_ref[...], p)).astype(BF16)
        tail_ref[p, :2 * t, :] = stack(lanes(bh, p)).astype(BF16)
        tail_ref[p, 2 * t:, :] = stack(lanes(kh, p)).astype(BF16)

    n_ab, n_ak, b_r = [], [], []
    for p in pairs:
        sc = _dot_t(lhs_ref[p], rhs_ref[p])
        n_ab.append(jnp.where(strict, sc[:2 * t, :2 * t], 0.0))
        n_ak.append(jnp.where(strict, sc[:2 * t, 2 * t:], 0.0).astype(BF16))
        b_r.append(jnp.concatenate([jnp.where(incl, sc[2 * t:, :2 * t], 0.0),
                                    jnp.where(incl, sc[2 * t:, 2 * t:], 0.0)], axis=1).astype(BF16))

    inv = [eye + n for n in n_ab]
    pw = n_ab
    for _ in range(n_doublings - 1):
        pwb = [x.astype(BF16) for x in pw]
        pw = [mm(x, x) for x in pwb]
        inv = [i_p + mm(q.astype(BF16), i_p.astype(BF16)) for i_p, q in zip(inv, pw)]
    inv = [x.astype(BF16) for x in inv]

    state = [state_ref[p] for p in pairs]
    state_b = [s.astype(BF16) for s in state]
    w = [_dot_t(lhs_ref[p, :2 * t, :], state_b[p]) + mm(n_ak[p], vs_ref[p]) for p in pairs]
    zv = [jnp.concatenate([mm(inv[p], w[p].astype(BF16)).astype(BF16), vs_ref[p]], axis=0) for p in pairs]
    y_parts = []
    for p in pairs:
        y_s = _dot_t(lhs_ref[p, 2 * t:, :], state_b[p]) + mm(b_r[p], zv[p])
        y_parts.append(y_s[:t, :] + y_s[t:, :])
        state_ref[p] = state[p] * lanes(g_last, p) + _t_dot(zv[p], tail_ref[p])
    y = jnp.concatenate(y_parts, axis=1)

    inv_n = 1.0 / RWKV_HEAD_SIZE
    mean = head_sum(y) * inv_n
    yc = y - mean
    var = head_sum(yc * yc) * inv_n
    yn = yc * lax.rsqrt(var + RWKV_GN_EPS) * lng_ref[...] + lnb_ref[...]
    bonus = head_sum(r * k * rk_ref[...]) * v_ref[...]
    o_ref[...] = ((yn + bonus) * g_ref[...]).astype(o_ref.dtype)


def rwkv_scan(rkv, a_gate, log_w, gate, k_k, k_a, r_k, ln_g, ln_b, batch, seq):
    _, m, d = rkv.shape
    t = SCAN_CHUNK
    nc = seq // t
    n_pairs = d // LANES
    row = lambda g: pl.BlockSpec((None, t, d), lambda b, c, g=g: (g, b * nc + c, 0))
    tile = pl.BlockSpec((t, d), lambda b, c: (b * nc + c, 0))
    vec = pl.BlockSpec((1, d), lambda b, c: (0, 0))
    return pl.pallas_call(
        _scan_chunk_body,
        out_shape=jax.ShapeDtypeStruct((m, d), BF16),
        grid=(batch, nc),
        in_specs=[row(0), row(1), row(2), tile, tile, tile, vec, vec, vec, vec, vec],
        out_specs=tile,
        scratch_shapes=[pltpu.VMEM((n_pairs, LANES, LANES), F32),
                        pltpu.VMEM((n_pairs, 4 * t, LANES), BF16),
                        pltpu.VMEM((n_pairs, 4 * t, LANES), BF16),
                        pltpu.VMEM((n_pairs, 2 * t, LANES), BF16),
                        pltpu.VMEM((n_pairs, 4 * t, LANES), BF16)],
        compiler_params=_params(("arbitrary", "arbitrary")),
        name="rwkv_scan",
    )(rkv, rkv, rkv, a_gate, log_w, gate,
      k_k.reshape(1, d), k_a.reshape(1, d), r_k.reshape(1, d), ln_g.reshape(1, d), ln_b.reshape(1, d))


def _alibi_slopes(n_groups, n_heads):
    n = n_groups * n_heads
    idx = jnp.arange(1, n + 1, dtype=F32)
    return (2.0 ** (-8.0 * idx / n)).reshape(n_groups, n_heads)


def kernel(x, p, attn_norm, attn_w_qkv, attn_w_o, rwkv_norm, rwkv_mu, rwkv_w_rkv, rwkv_w0, rwkv_w_w1, rwkv_w_w2, rwkv_a0, rwkv_w_a1, rwkv_w_a2, rwkv_w_g1, rwkv_w_g2, rwkv_k_k, rwkv_k_a, rwkv_r_k, rwkv_ln_g, rwkv_ln_b, rwkv_w_o, conv_norm, conv_w_in, conv_w, conv_w_out, ffn_norm, ffn_w_gu, ffn_conv_w, ffn_conv_b, ffn_w_down, ple_w_proj, ple_norm, ple_w_gate, final_norm):
    batch, seq, d = x.shape
    depth = p.shape[0]
    m = batch * seq
    n_groups = len(ATTN_GROUPS)
    slopes = _alibi_slopes(n_groups, ATTN_HEADS)
    x = x.reshape(m, d)
    p = p.reshape(depth, m, p.shape[-1])

    h_next = None
    for i in range(depth):
        kind, j = i % 3, i // 3
        if kind == 0:
            hs = attn_rmsnorm(x, attn_norm[j], batch, seq)
            n_q = ATTN_HEADS * ATTN_HEAD_DIM
            q_scale = jnp.concatenate([jnp.full((1, n_q), ATTN_HEAD_DIM ** -0.5 * LOG2_E, F32),
                                       jnp.ones((1, 2 * n_q), F32)], axis=1)
            qkv = [linear(hs[g], attn_w_qkv, j, BF16, tm=2048, n_out=3 * n_q, first_col=g * 3 * n_q,
                          col_scale=q_scale, name="attn_qkv") for g in range(n_groups)]
            y = dilated_attention(qkv, slopes, batch, seq)
            w_out = attn_w_o
        elif kind == 1:
            xs = rwkv_prep(x, rwkv_norm[j], rwkv_mu[j], seq)
            rkv = grouped_linear(xs, rwkv_w_rkv, j, F32, tm=2048, tn=512, name="rwkv_rkv")
            log_w = lora(xs, 3, rwkv_w_w1[j], rwkv_w_w2[j], "decay", rwkv_w0[j])
            a_gate = lora(xs, 4, rwkv_w_a1[j], rwkv_w_a2[j], "iclr", rwkv_a0[j])
            gate = lora(xs, 5, rwkv_w_g1[j], rwkv_w_g2[j], "gate")
            y = rwkv_scan(rkv, a_gate, log_w, gate, rwkv_k_k[j], rwkv_k_a[j], rwkv_r_k[j].reshape(d),
                          rwkv_ln_g[j], rwkv_ln_b[j], batch, seq)
            w_out = rwkv_w_o
        else:
            y = shortconv_in(h_next, conv_w_in, j, conv_w[j], seq, tm=2048, tn=256)
            w_out = conv_w_out
        x, h = project_residual_norm(y, w_out, j, x, ffn_norm[i])
        act = ffn_gate_up(h, ffn_w_gu, i, ffn_conv_w[i], ffn_conv_b[i], seq, tm=2048)
        x = linear(act, ffn_w_down, i, F32, tm=512, tn=512, residual=x, name="ffn_down")
        if i + 1 == depth:
            _, out = ple_update(x, p, i, ple_w_gate, ple_w_proj, ple_norm[i], final_norm, F32)
            return out.reshape(batch, seq, d)
        if (i + 1) % 3 == 2:
            x, h_next = ple_update(x, p, i, ple_w_gate, ple_w_proj, ple_norm[i], conv_norm[(i + 1) // 3])
        else:
            x = ple_update(x, p, i, ple_w_gate, ple_w_proj, ple_norm[i])
```

```python
import functools

import jax
import jax.numpy as jnp
from jax import lax
from jax.experimental import pallas as pl
from jax.experimental.pallas import tpu as pltpu

F32 = jnp.float32
BF16 = jnp.bfloat16

NORM_EPS = 1e-6
NEG_INF = -1e30
RWKV_GN_EPS = 6.4e-4

ATTN_GROUPS = ((128, 1), (512, 4), (2048, 16))
ATTN_HEADS = 16
ATTN_HEAD_DIM = 128
ATTN_BLOCK = 128
ATTN_BLOCKS_IN_FLIGHT = 16
LOG2_E = 1.4426950408889634
EXP_MINUS_HALF = 0.6065306597126334
RWKV_HEAD_SIZE = 64
SCAN_CHUNK = 64
LANES = 128
CARRY_ROWS = 8
SUB_ROWS = 256
SUB_COLS = 512

VMEM_LIMIT_BYTES = 56 * 1024 * 1024


def _params(semantics):
    return pltpu.CompilerParams(dimension_semantics=semantics, vmem_limit_bytes=VMEM_LIMIT_BYTES)


def _sigmoid(x):
    return 1.0 / (1.0 + jnp.exp(-x))


def _rms(x, g):
    ms = jnp.mean(x * x, axis=-1, keepdims=True)
    return x * lax.rsqrt(ms + NORM_EPS) * g


def _shift_rows(cur, prev_tail, n):
    row = lax.broadcasted_iota(jnp.int32, cur.shape, 0)
    out = pltpu.roll(cur, n, axis=0)
    for r in range(n):
        src = CARRY_ROWS - n + r
        out = jnp.where(row == r, prev_tail[src:src + 1, :], out)
    return out


def _attn_norm_body(x_ref, g_ref, o0_ref, o1_ref, o2_ref, y_ref, *, tm):
    y = _rms(x_ref[...], g_ref[...])
    o0_ref[...] = y.astype(o0_ref.dtype)
    for c in range(y_ref.shape[0]):
        y_ref[c] = y[:, c * LANES:(c + 1) * LANES]
    for o_ref, dil in ((o1_ref, ATTN_GROUPS[1][1]), (o2_ref, ATTN_GROUPS[2][1])):
        for r in range(dil):
            for c in range(y_ref.shape[0]):
                o_ref[r, :, c * LANES:(c + 1) * LANES] = (
                    y_ref[c, pl.ds(r, tm // dil, stride=dil), :].astype(o_ref.dtype))


def attn_rmsnorm(x, g, batch, seq, tm=256):
    m, d = x.shape
    bps = seq // tm
    d1, d2 = ATTN_GROUPS[1][1], ATTN_GROUPS[2][1]
    outs = pl.pallas_call(
        functools.partial(_attn_norm_body, tm=tm),
        out_shape=(jax.ShapeDtypeStruct((m, d), BF16),
                   jax.ShapeDtypeStruct((batch, d1, seq // d1, d), BF16),
                   jax.ShapeDtypeStruct((batch, d2, seq // d2, d), BF16)),
        grid=(batch, bps),
        in_specs=[pl.BlockSpec((tm, d), lambda b, i: (b * bps + i, 0)),
                  pl.BlockSpec((1, d), lambda b, i: (0, 0))],
        out_specs=(pl.BlockSpec((tm, d), lambda b, i: (b * bps + i, 0)),
                   pl.BlockSpec((None, d1, tm // d1, d), lambda b, i: (b, 0, i, 0)),
                   pl.BlockSpec((None, d2, tm // d2, d), lambda b, i: (b, 0, i, 0))),
        scratch_shapes=[pltpu.VMEM((d // LANES, tm, LANES), F32)],
        compiler_params=_params(("arbitrary", "arbitrary")),
        name="attn_norm",
    )(x, g.reshape(1, d))
    return [o.reshape(m, d) for o in outs]


def _rwkv_prep_body(x_ref, g_ref, mu_ref, w0_ref, a0_ref, *refs, bps):
    lora_refs = refs[:6]
    rkv_ref, lw_ref, a_ref, gate_ref = refs[6:10]
    carry_ref = refs[10]
    lora_b = refs[11:17]

    @pl.when(pl.program_id(0) == 0)
    def _():
        for w_ref, wb_ref in zip(lora_refs, lora_b):
            _cast_weight(w_ref, wb_ref)

    @pl.when(pl.program_id(0) % bps == 0)
    def _():
        carry_ref[...] = jnp.zeros_like(carry_ref)

    h = _rms(x_ref[...], g_ref[...])
    xx = _shift_rows(h, carry_ref[...], 1) - h
    carry_ref[...] = h[h.shape[0] - CARRY_ROWS:, :]

    def mix(n):
        return (h + xx * mu_ref[n:n + 1, :]).astype(BF16)

    def low_rank(n, w1b_ref, w2b_ref, mid_fn):
        mid = mid_fn(jnp.dot(mix(n), w1b_ref[...], preferred_element_type=F32))
        return jnp.dot(mid.astype(BF16), w2b_ref[...], preferred_element_type=F32)

    for n in range(3):
        rkv_ref[n] = mix(n)
    u = w0_ref[...] + low_rank(3, lora_b[0], lora_b[1], jnp.tanh)
    lw_ref[...] = -EXP_MINUS_HALF * _sigmoid(u)
    a_ref[...] = _sigmoid(a0_ref[...] + low_rank(4, lora_b[2], lora_b[3], lambda t: t))
    gate_ref[...] = low_rank(5, lora_b[4], lora_b[5], _sigmoid)


def rwkv_prep(x, g, mu, w0, a0, loras, seq, tm=256):
    m, d = x.shape
    padded = []
    for idx in range(0, len(loras), 2):
        w1, w2 = loras[idx], loras[idx + 1]
        r = -(-w1.shape[1] // LANES) * LANES
        padded += [jnp.pad(w1, ((0, 0), (0, r - w1.shape[1]))), jnp.pad(w2, ((0, r - w2.shape[0]), (0, 0)))]
    row = pl.BlockSpec((tm, d), lambda i: (i, 0))
    vec = pl.BlockSpec((1, d), lambda i: (0, 0))
    whole = lambda a: pl.BlockSpec(a.shape, lambda i: (0,) * a.ndim, pipeline_mode=pl.Buffered(1))
    return pl.pallas_call(
        functools.partial(_rwkv_prep_body, bps=seq // tm),
        out_shape=(jax.ShapeDtypeStruct((3, m, d), BF16),) + (jax.ShapeDtypeStruct((m, d), F32),) * 3,
        grid=(m // tm,),
        in_specs=[row, vec, whole(mu), vec, vec] + [whole(w) for w in padded],
        out_specs=(pl.BlockSpec((3, tm, d), lambda i: (0, i, 0)), row, row, row),
        scratch_shapes=[pltpu.VMEM((CARRY_ROWS, d), F32)] + [pltpu.VMEM(w.shape, BF16) for w in padded],
        compiler_params=_params(("arbitrary",)),
        name="rwkv_prep",
    )(x, g.reshape(1, d), mu, w0.reshape(1, d), a0.reshape(1, d), *padded)


def _cast_weight(w_ref, wb_ref, rows=256):
    k = w_ref.shape[0]
    rows = min(rows, k)

    def chunk(c, carry):
        sl = pl.ds(pl.multiple_of(c * rows, rows), rows)
        wb_ref[sl, :] = w_ref[sl, :].astype(wb_ref.dtype)
        return carry

    lax.fori_loop(0, k // rows, chunk, 0)


def _mm_body(*refs, n_x, w_lhs, n_extra, n_scratch, epilogue):
    n_w = len(w_lhs)
    x_refs = refs[:n_x]
    w_refs = refs[n_x:n_x + n_w]
    e_refs = refs[n_x + n_w:n_x + n_w + n_extra]
    o_ref = refs[n_x + n_w + n_extra]
    wb_refs = refs[n_x + n_w + n_extra + 1:n_x + 2 * n_w + n_extra + 1]
    s_refs = refs[n_x + 2 * n_w + n_extra + 1:]
    assert len(s_refs) == n_scratch

    @pl.when(pl.program_id(2) == 0)
    def _():
        for w_ref, wb_ref in zip(w_refs, wb_refs):
            _cast_weight(w_ref, wb_ref)

    tm = o_ref.shape[0]
    sub = min(SUB_ROWS, tm)
    for c in range(tm // sub):
        rows = pl.ds(c * sub, sub)
        xs = [x_ref[rows, :].astype(BF16) for x_ref in x_refs]
        accs = [jnp.dot(xs[l], wb_ref[...], preferred_element_type=F32)
                for wb_ref, l in zip(wb_refs, w_lhs)]
        o_ref[rows, :] = epilogue(accs, e_refs, s_refs, rows, c == 0).astype(o_ref.dtype)


def matmul(xs, ws, extras, epilogue, out_shape, out_spec, grid, scratch=(), name="matmul"):
    w_lhs = tuple(l for _, _, l in ws)
    wb_scratch = [pltpu.VMEM(spec.block_shape[-2:], BF16) for _, spec, _ in ws]
    body = functools.partial(_mm_body, n_x=len(xs), w_lhs=w_lhs, n_extra=len(extras),
                             n_scratch=len(scratch), epilogue=epilogue)
    return pl.pallas_call(
        body,
        out_shape=out_shape,
        grid=grid,
        in_specs=[s for _, s in xs] + [s for _, s, _ in ws] + [s for _, s in extras],
        out_specs=out_spec,
        scratch_shapes=wb_scratch + list(scratch),
        compiler_params=_params(("arbitrary", "arbitrary", "arbitrary")),
        name=name,
    )(*[a for a, _ in xs], *[a for a, _, _ in ws], *[a for a, _ in extras])


def _ep_plain(accs, e_refs, s_refs, rows, first_sub):
    return accs[0]


def _ep_residual(accs, e_refs, s_refs, rows, first_sub):
    return e_refs[0][rows, :] + accs[0]


def _ep_col_scale(accs, e_refs, s_refs, rows, first_sub):
    return accs[0] * e_refs[0][...]


def _conv3(cur, cw_ref, carry_ref, bps, first_sub):
    if first_sub:
        @pl.when(pl.program_id(2) % bps == 0)
        def _():
            carry_ref[...] = jnp.zeros_like(carry_ref)

    tail = carry_ref[...]
    d1 = _shift_rows(cur, tail, 1)
    d2 = _shift_rows(cur, tail, 2)
    carry_ref[...] = cur[cur.shape[0] - CARRY_ROWS:, :]
    w = cw_ref[...]
    return d2 * w[0:1, :] + d1 * w[1:2, :] + cur * w[2:3, :]


def _ep_ffn(accs, e_refs, s_refs, rows, first_sub, *, bps):
    gate = _conv3(accs[0], e_refs[0], s_refs[0], bps, first_sub) + e_refs[1][...]
    return gate * _sigmoid(gate) * accs[1]


def _ep_shortconv(accs, e_refs, s_refs, rows, first_sub, *, bps):
    return accs[0] * _conv3(accs[1] * accs[2], e_refs[0], s_refs[0], bps, first_sub)


def _layer_cols(w, layer, tn, first_block=0):
    k = w.shape[1]
    return pl.BlockSpec((None, k, tn), lambda g, j, i: (layer, 0, first_block + j))


def linear(x, w, layer, out_dtype, tm=512, tn=1024, n_out=None, first_col=0, residual=None, col_scale=None,
           name="linear"):
    m, k = x.shape
    n = n_out or w.shape[2]
    tn = min(tn, n)
    extras = []
    epilogue = _ep_plain
    assert residual is None or col_scale is None
    if residual is not None:
        extras = [(residual, pl.BlockSpec((tm, tn), lambda g, j, i: (i, j)))]
        epilogue = _ep_residual
    if col_scale is not None:
        extras = [(col_scale, pl.BlockSpec((1, tn), lambda g, j, i: (0, j)))]
        epilogue = _ep_col_scale
    return matmul(
        [(x, pl.BlockSpec((tm, k), lambda g, j, i: (i, 0)))],
        [(w, _layer_cols(w, layer, tn, first_col // tn), 0)],
        extras, epilogue,
        jax.ShapeDtypeStruct((m, n), out_dtype),
        pl.BlockSpec((tm, tn), lambda g, j, i: (i, j)),
        (1, n // tn, m // tm), name=name)


def grouped_linear(x, w, layer, out_dtype, tm=512, tn=1024, name="grouped_linear"):
    m, k = x.shape[1:]
    n_groups, _, n_out = w.shape[1:]
    tn = min(tn, n_out)
    return matmul(
        [(x, pl.BlockSpec((None, tm, k), lambda g, j, i: (g, i, 0)))],
        [(w, pl.BlockSpec((None, None, k, tn), lambda g, j, i: (layer, g, 0, j)), 0)],
        [], _ep_plain,
        jax.ShapeDtypeStruct((n_groups, m, n_out), out_dtype),
        pl.BlockSpec((None, tm, tn), lambda g, j, i: (g, i, j)),
        (n_groups, n_out // tn, m // tm), name=name)


def ffn_gate_up(h, w_gu, layer, conv_w, conv_b, seq, tm=512, tn=512):
    m, k = h.shape
    f = w_gu.shape[2] // 2
    nj = f // tn
    vec = lambda rows: pl.BlockSpec((rows, tn), lambda g, j, i: (0, j))
    return matmul(
        [(h, pl.BlockSpec((tm, k), lambda g, j, i: (i, 0)))],
        [(w_gu, _layer_cols(w_gu, layer, tn), 0),
         (w_gu, _layer_cols(w_gu, layer, tn, nj), 0)],
        [(conv_w, vec(conv_w.shape[0])), (conv_b.reshape(1, f), vec(1))],
        functools.partial(_ep_ffn, bps=seq // tm),
        jax.ShapeDtypeStruct((m, f), BF16),
        pl.BlockSpec((tm, tn), lambda g, j, i: (i, j)),
        (1, nj, m // tm),
        scratch=[pltpu.VMEM((CARRY_ROWS, tn), F32)], name="ffn_gate_up")


def shortconv_in(h, w_in, layer, conv_w, seq, tm=512, tn=512):
    m, k = h.shape
    d = w_in.shape[2] // 3
    nj = d // tn
    return matmul(
        [(h, pl.BlockSpec((tm, k), lambda g, j, i: (i, 0)))],
        [(w_in, _layer_cols(w_in, layer, tn, s * nj), 0) for s in range(3)],
        [(conv_w, pl.BlockSpec((conv_w.shape[0], tn), lambda g, j, i: (0, j)))],
        functools.partial(_ep_shortconv, bps=seq // tm),
        jax.ShapeDtypeStruct((m, d), BF16),
        pl.BlockSpec((tm, tn), lambda g, j, i: (i, j)),
        (1, nj, m // tm),
        scratch=[pltpu.VMEM((CARRY_ROWS, tn), F32)], name="shortconv_in")


def _whole_weight(w, layer):
    return pl.BlockSpec((None,) + w.shape[1:], lambda i: (layer, 0, 0), pipeline_mode=pl.Buffered(1))


def _norm_rows(o_ref, ssq, g_ref, h_ref):
    n = o_ref.shape[1]
    rstd = lax.rsqrt(ssq * (1.0 / n) + NORM_EPS)
    for c in range(n // SUB_COLS):
        cols = pl.ds(c * SUB_COLS, SUB_COLS)
        h_ref[:, cols] = (o_ref[:, cols] * rstd * g_ref[:, cols]).astype(h_ref.dtype)


def _proj_norm_body(y_ref, w_ref, res_ref, g_ref, o_ref, h_ref, wb_ref):
    @pl.when(pl.program_id(0) == 0)
    def _():
        _cast_weight(w_ref, wb_ref)

    y = y_ref[...]
    ssq = jnp.zeros((y.shape[0], 1), F32)
    for c in range(o_ref.shape[1] // SUB_COLS):
        cols = pl.ds(c * SUB_COLS, SUB_COLS)
        xn = res_ref[:, cols] + jnp.dot(y, wb_ref[:, cols], preferred_element_type=F32)
        o_ref[:, cols] = xn
        ssq = ssq + jnp.sum(xn * xn, axis=-1, keepdims=True)
    _norm_rows(o_ref, ssq, g_ref, h_ref)


def project_residual_norm(y, w, layer, residual, g, tm=512):
    m, k = y.shape
    n = w.shape[2]
    row = lambda width: pl.BlockSpec((tm, width), lambda i: (i, 0))
    return pl.pallas_call(
        _proj_norm_body,
        out_shape=(jax.ShapeDtypeStruct((m, n), F32), jax.ShapeDtypeStruct((m, n), BF16)),
        grid=(m // tm,),
        in_specs=[row(k), _whole_weight(w, layer), row(n), pl.BlockSpec((1, n), lambda i: (0, 0))],
        out_specs=(row(n), row(n)),
        scratch_shapes=[pltpu.VMEM((k, n), BF16)],
        compiler_params=_params(("arbitrary",)),
        name="project_residual_norm",
    )(y, w, residual, g.reshape(1, n))


def _ple_body(x_ref, p_ref, wg_ref, wp_ref, g_ref, *refs, norm_next):
    if norm_next:
        gn_ref, o_ref, h_ref, wgb_ref, wpb_ref = refs
    else:
        o_ref, wgb_ref, wpb_ref = refs

    @pl.when(pl.program_id(0) == 0)
    def _():
        _cast_weight(wg_ref, wgb_ref)
        _cast_weight(wp_ref, wpb_ref)

    hn = _rms(x_ref[...], g_ref[...]).astype(BF16)
    pb = p_ref[...].astype(BF16)
    ssq = jnp.zeros((hn.shape[0], 1), F32)
    for c in range(o_ref.shape[1] // SUB_COLS):
        cols = pl.ds(c * SUB_COLS, SUB_COLS)
        gate = jnp.dot(hn, wgb_ref[:, cols], preferred_element_type=F32)
        proj = jnp.dot(pb, wpb_ref[:, cols], preferred_element_type=F32)
        xn = x_ref[:, cols] + _sigmoid(gate) * proj
        o_ref[:, cols] = xn
        ssq = ssq + jnp.sum(xn * xn, axis=-1, keepdims=True)
    if norm_next:
        _norm_rows(o_ref, ssq, gn_ref, h_ref)


def ple_update(x, p, layer, w_gate, w_proj, g, g_next=None, next_dtype=BF16, tm=512):
    m, d = x.shape
    kp = p.shape[2]
    norm_next = g_next is not None
    row = pl.BlockSpec((tm, d), lambda i: (i, 0))
    vec = pl.BlockSpec((1, d), lambda i: (0, 0))
    operands = [x, p, w_gate, w_proj, g.reshape(1, d)] + ([g_next.reshape(1, d)] if norm_next else [])
    in_specs = [row, pl.BlockSpec((None, tm, kp), lambda i: (layer, i, 0)),
                _whole_weight(w_gate, layer), _whole_weight(w_proj, layer), vec, vec]
    out_shape = [jax.ShapeDtypeStruct((m, d), F32), jax.ShapeDtypeStruct((m, d), next_dtype)]
    n_out = 2 if norm_next else 1
    outs = pl.pallas_call(
        functools.partial(_ple_body, norm_next=norm_next),
        out_shape=tuple(out_shape[:n_out]),
        grid=(m // tm,),
        in_specs=in_specs[:len(operands)],
        out_specs=(row,) * n_out,
        scratch_shapes=[pltpu.VMEM((d, d), BF16), pltpu.VMEM((kp, d), BF16)],
        compiler_params=_params(("arbitrary",)),
        name="ple_update",
    )(*operands)
    return tuple(outs) if norm_next else outs[0]


def _attn_body(slopes_ref, *refs):
    qkv_refs = refs[:9]
    o_ref = refs[9]
    acc_ref, m_ref, l_ref = refs[10:]
    head = pl.program_id(1)
    blk = ATTN_BLOCK

    qi = lax.broadcasted_iota(jnp.int32, (blk, 2 * blk), 0)
    kj = lax.broadcasted_iota(jnp.int32, (blk, 2 * blk), 1)
    dist = qi + blk - kj
    distf = dist.astype(F32)

    blocks = []
    for g, (window, dil) in enumerate(ATTN_GROUPS):
        seq = qkv_refs[3 * g].shape[0]
        nb = seq // dil // blk
        n_back = window // dil
        coef = slopes_ref[g, head] * (float(dil) * LOG2_E)
        bias2 = jnp.where((dist >= 0) & (dist <= n_back), -(coef * distf), NEG_INF)
        bias1 = bias2[:, blk:]
        for j in range(seq // blk):
            blocks.append((g, dil, j // nb, j % nb, j, bias1, bias2))

    for b0 in range(0, len(blocks), ATTN_BLOCKS_IN_FLIGHT):
        batch = blocks[b0:b0 + ATTN_BLOCKS_IN_FLIGHT]
        s_list, v_list = [], []
        for g, dil, r, n, j, bias1, bias2 in batch:
            q_ref, k_ref, v_ref = qkv_refs[3 * g:3 * g + 3]
            lo = j * blk if n == 0 else (j - 1) * blk
            s = lax.dot_general(q_ref[j * blk:(j + 1) * blk, :], k_ref[lo:(j + 1) * blk, :],
                                (((1,), (1,)), ((), ())), preferred_element_type=F32)
            s_list.append(s + (bias1 if n == 0 else bias2))
            v_list.append(v_ref[lo:(j + 1) * blk, :])
        m_list = [jnp.max(s, axis=1, keepdims=True) for s in s_list]
        p_list = [jnp.exp2(s - m) for s, m in zip(s_list, m_list)]
        l_list = [jnp.sum(p, axis=1, keepdims=True) for p in p_list]
        acc_list = [jnp.dot(p.astype(BF16), v, preferred_element_type=F32) for p, v in zip(p_list, v_list)]
        for (g, dil, r, n, j, _, _), m, l, acc in zip(batch, m_list, l_list, acc_list):
            start = n * blk * dil + r
            rows = pl.ds(start, blk) if dil == 1 else pl.ds(start, blk, stride=dil)
            acc_ref[g, rows, :] = acc
            m_ref[g, rows, :] = jnp.broadcast_to(m, acc.shape)
            l_ref[g, rows, :] = jnp.broadcast_to(l, acc.shape)

    rows_per_step = 256
    for c in range(o_ref.shape[0] // rows_per_step):
        sl = pl.ds(c * rows_per_step, rows_per_step)
        ms = [m_ref[g, sl, :] for g in range(3)]
        top = jnp.maximum(jnp.maximum(ms[0], ms[1]), ms[2])
        es = [jnp.exp2(mg - top) for mg in ms]
        num = es[0] * acc_ref[0, sl, :] + es[1] * acc_ref[1, sl, :] + es[2] * acc_ref[2, sl, :]
        den = es[0] * l_ref[0, sl, :] + es[1] * l_ref[1, sl, :] + es[2] * l_ref[2, sl, :]
        o_ref[sl, :] = (num / den).astype(o_ref.dtype)


def dilated_attention(qkv, slopes, batch, seq):
    n_g = len(qkv)
    m = qkv[0].shape[0]
    hd = ATTN_HEAD_DIM
    in_specs = [pl.BlockSpec(memory_space=pltpu.SMEM)]
    for g in range(n_g):
        for c in range(3):
            in_specs.append(pl.BlockSpec((seq, hd), lambda b, h, c=c: (b, c * ATTN_HEADS + h)))
    return pl.pallas_call(
        _attn_body,
        out_shape=jax.ShapeDtypeStruct((m, ATTN_HEADS * hd), BF16),
        grid=(batch, ATTN_HEADS),
        in_specs=in_specs,
        out_specs=pl.BlockSpec((seq, hd), lambda b, h: (b, h)),
        scratch_shapes=[pltpu.VMEM((n_g, seq, hd), F32)] * 3,
        compiler_params=_params(("arbitrary", "arbitrary")),
        name="dilated_attention",
    )(slopes, *[a for a in qkv for _ in range(3)])


def _split3(x):
    hi = x.astype(BF16)
    r1 = x - hi.astype(F32)
    mid = r1.astype(BF16)
    lo = (r1 - mid.astype(F32)).astype(BF16)
    return hi, mid, lo


def _dot_split2_rhs(x, rhs_bf16):
    hi = x.astype(BF16)
    lo = (x - hi.astype(F32)).astype(BF16)
    return jnp.dot(hi, rhs_bf16, preferred_element_type=F32) + jnp.dot(lo, rhs_bf16, preferred_element_type=F32)


def _dot_exact_lhs(lhs_bf16, x):
    return sum(jnp.dot(lhs_bf16, part, preferred_element_type=F32) for part in _split3(x))


def _dot_t(a, b):
    return lax.dot_general(a, b, (((1,), (1,)), ((), ())), preferred_element_type=F32)


def _t_dot(a, b):
    return lax.dot_general(a, b, (((0,), (0,)), ((), ())), preferred_element_type=F32)


def _scan_chunk_body(r_ref, k_ref, v_ref, a_ref, lw_ref, g_ref, kk_ref, ka_ref, rk_ref, lng_ref, lnb_ref,
                     o_ref, state_ref, lhs_ref, rhs_ref, vs_ref, tail_ref):
    t = r_ref.shape[0]
    n_pairs = r_ref.shape[1] // LANES
    pairs = range(n_pairs)

    @pl.when(pl.program_id(1) == 0)
    def _():
        state_ref[...] = jnp.zeros_like(state_ref)

    lane = lax.broadcasted_iota(jnp.int32, (t, LANES), 1)
    head0 = lane < RWKV_HEAD_SIZE
    ri = lax.broadcasted_iota(jnp.int32, (2 * t, 2 * t), 0)
    ci = lax.broadcasted_iota(jnp.int32, (2 * t, 2 * t), 1)
    same_head = (ri < t) == (ci < t)
    strict = same_head & (ci < ri)
    incl = same_head & (ci <= ri)
    eye = jnp.where(ri == ci, 1.0, 0.0)
    tri = jnp.where(lax.broadcasted_iota(jnp.int32, (t, t), 1) <= lax.broadcasted_iota(jnp.int32, (t, t), 0),
                    1.0, 0.0).astype(BF16)
    li = lax.broadcasted_iota(jnp.int32, (LANES, LANES), 0)
    lj = lax.broadcasted_iota(jnp.int32, (LANES, LANES), 1)
    head_ones = jnp.where((li < RWKV_HEAD_SIZE) == (lj < RWKV_HEAD_SIZE), 1.0, 0.0).astype(BF16)
    zeros = jnp.zeros_like(head_ones)
    head_ones2 = jnp.concatenate([jnp.concatenate([head_ones, zeros], axis=1),
                                  jnp.concatenate([zeros, head_ones], axis=1)], axis=0)
    n_doublings = (t - 1).bit_length()

    def lanes(x, p):
        return x[:, p * LANES:(p + 1) * LANES]

    def stack(x):
        return jnp.concatenate([jnp.where(head0, x, 0.0), jnp.where(head0, 0.0, x)], axis=0)

    def tall(x):
        return jnp.concatenate([lanes(x, p) for p in pairs], axis=0)

    def wide(s):
        return jnp.concatenate([s[p * t:(p + 1) * t, :] for p in pairs], axis=1)

    def head_sum(x):
        return wide(_dot_split2_rhs(tall(x), head_ones))

    def head_sum_two(x1, x2):
        s = _dot_split2_rhs(jnp.concatenate([tall(x1), tall(x2)], axis=1), head_ones2)
        return wide(s[:, :LANES]), wide(s[:, LANES:])

    def mm(a, b):
        return jnp.dot(a, b, preferred_element_type=F32)

    r = r_ref[...]
    k0 = k_ref[...]
    a_gate = a_ref[...]
    lw = lw_ref[...]
    kk = k0 * kk_ref[...]
    k = k0 * (1.0 + (a_gate - 1.0) * ka_ref[...])
    kk_sq, rk_sum = head_sum_two(kk * kk, r * k * rk_ref[...])
    kk = kk / jnp.maximum(jnp.sqrt(kk_sq), 1e-12)
    b = kk * a_gate
    lg = _dot_exact_lhs(tri, lw)
    g_incl = jnp.exp(lg)
    g_inv = jnp.exp(-lg)
    g_last = g_incl[t - 1:t, :]
    g_tail = g_last * g_inv
    at = -kk * jnp.exp(lg - lw)
    rt = r * g_incl
    bt = b * g_inv
    kt = k * g_inv
    bh = b * g_tail
    kh = k * g_tail
    for p in pairs:
        lhs_ref[p, :2 * t, :] = stack(lanes(at, p)).astype(BF16)
        lhs_ref[p, 2 * t:, :] = stack(lanes(rt, p)).astype(BF16)
        btp = lanes(bt, p).astype(BF16)
        ktp = lanes(kt, p).astype(BF16)
        rhs_ref[p] = jnp.concatenate([btp, btp, ktp, ktp], axis=0)
        vs_ref[p] = stack(lanes(v_ref[...], p)).astype(BF16)
        tail_ref[p, :2 * t, :] = stack(lanes(bh, p)).astype(BF16)
        tail_ref[p, 2 * t:, :] = stack(lanes(kh, p)).astype(BF16)

    n_ab, n_ak, b_r = [], [], []
    for p in pairs:
        sc = _dot_t(lhs_ref[p], rhs_ref[p])
        n_ab.append(jnp.where(strict, sc[:2 * t, :2 * t], 0.0))
        n_ak.append(jnp.where(strict, sc[:2 * t, 2 * t:], 0.0).astype(BF16))
        b_r.append(jnp.concatenate([jnp.where(incl, sc[2 * t:, :2 * t], 0.0),
                                    jnp.where(incl, sc[2 * t:, 2 * t:], 0.0)], axis=1).astype(BF16))

    inv = [eye + n for n in n_ab]
    pw = n_ab
    for _ in range(n_doublings - 1):
        pwb = [x.astype(BF16) for x in pw]
        pw = [mm(x, x) for x in pwb]
        inv = [i_p + mm(q.astype(BF16), i_p.astype(BF16)) for i_p, q in zip(inv, pw)]
    inv = [x.astype(BF16) for x in inv]

    state = [state_ref[p] for p in pairs]
    state_b = [s.astype(BF16) for s in state]
    w = [_dot_t(lhs_ref[p, :2 * t, :], state_b[p]) + mm(n_ak[p], vs_ref[p]) for p in pairs]
    zv = [jnp.concatenate([mm(inv[p], w[p].astype(BF16)).astype(BF16), vs_ref[p]], axis=0) for p in pairs]
    y_parts = []
    for p in pairs:
        y_s = _dot_t(lhs_ref[p, 2 * t:, :], state_b[p]) + mm(b_r[p], zv[p])
        y_parts.append(y_s[:t, :] + y_s[t:, :])
        state_ref[p] = state[p] * lanes(g_last, p) + _t_dot(zv[p], tail_ref[p])
    y = jnp.concatenate(y_parts, axis=1)

    inv_n = 1.0 / RWKV_HEAD_SIZE
    mean = head_sum(y) * inv_n
    yc = y - mean
    var = head_sum(yc * yc) * inv_n
    yn = yc * lax.rsqrt(var + RWKV_GN_EPS) * lng_ref[...] + lnb_ref[...]
    bonus = rk_sum * v_ref[...]
    o_ref[...] = ((yn + bonus) * g_ref[...]).astype(o_ref.dtype)


def rwkv_scan(rkv, a_gate, log_w, gate, k_k, k_a, r_k, ln_g, ln_b, batch, seq):
    _, m, d = rkv.shape
    t = SCAN_CHUNK
    nc = seq // t
    n_pairs = d // LANES
    row = lambda g: pl.BlockSpec((None, t, d), lambda b, c, g=g: (g, b * nc + c, 0))
    tile = pl.BlockSpec((t, d), lambda b, c: (b * nc + c, 0))
    vec = pl.BlockSpec((1, d), lambda b, c: (0, 0))
    return pl.pallas_call(
        _scan_chunk_body,
        out_shape=jax.ShapeDtypeStruct((m, d), BF16),
        grid=(batch, nc),
        in_specs=[row(0), row(1), row(2), tile, tile, tile, vec, vec, vec, vec, vec],
        out_specs=tile,
        scratch_shapes=[pltpu.VMEM((n_pairs, LANES, LANES), F32),
                        pltpu.VMEM((n_pairs, 4 * t, LANES), BF16),
                        pltpu.VMEM((n_pairs, 4 * t, LANES), BF16),
                        pltpu.VMEM((n_pairs, 2 * t, LANES), BF16),
                        pltpu.VMEM((n_pairs, 4 * t, LANES), BF16)],
        compiler_params=_params(("arbitrary", "arbitrary")),
        name="rwkv_scan",
    )(rkv, rkv, rkv, a_gate, log_w, gate,
      k_k.reshape(1, d), k_a.reshape(1, d), r_k.reshape(1, d), ln_g.reshape(1, d), ln_b.reshape(1, d))


def _alibi_slopes(n_groups, n_heads):
    n = n_groups * n_heads
    idx = jnp.arange(1, n + 1, dtype=F32)
    return (2.0 ** (-8.0 * idx / n)).reshape(n_groups, n_heads)


def kernel(x, p, attn_norm, attn_w_qkv, attn_w_o, rwkv_norm, rwkv_mu, rwkv_w_rkv, rwkv_w0, rwkv_w_w1, rwkv_w_w2, rwkv_a0, rwkv_w_a1, rwkv_w_a2, rwkv_w_g1, rwkv_w_g2, rwkv_k_k, rwkv_k_a, rwkv_r_k, rwkv_ln_g, rwkv_ln_b, rwkv_w_o, conv_norm, conv_w_in, conv_w, conv_w_out, ffn_norm, ffn_w_gu, ffn_conv_w, ffn_conv_b, ffn_w_down, ple_w_proj, ple_norm, ple_w_gate, final_norm):
    batch, seq, d = x.shape
    depth = p.shape[0]
    m = batch * seq
    n_groups = len(ATTN_GROUPS)
    slopes = _alibi_slopes(n_groups, ATTN_HEADS)
    x = x.reshape(m, d)
    p = p.reshape(depth, m, p.shape[-1])

    h_next = None
    for i in range(depth):
        kind, j = i % 3, i // 3
        if kind == 0:
            hs = attn_rmsnorm(x, attn_norm[j], batch, seq)
            n_q = ATTN_HEADS * ATTN_HEAD_DIM
            q_scale = jnp.concatenate([jnp.full((1, n_q), ATTN_HEAD_DIM ** -0.5 * LOG2_E, F32),
                                       jnp.ones((1, 2 * n_q), F32)], axis=1)
            qkv = [linear(hs[g], attn_w_qkv, j, BF16, tm=2048, n_out=3 * n_q, first_col=g * 3 * n_q,
                          col_scale=q_scale, name="attn_qkv") for g in range(n_groups)]
            y = dilated_attention(qkv, slopes, batch, seq)
            w_out = attn_w_o
        elif kind == 1:
            xs, log_w, a_gate, gate = rwkv_prep(
                x, rwkv_norm[j], rwkv_mu[j], rwkv_w0[j], rwkv_a0[j],
                [rwkv_w_w1[j], rwkv_w_w2[j], rwkv_w_a1[j], rwkv_w_a2[j], rwkv_w_g1[j], rwkv_w_g2[j]], seq)
            rkv = grouped_linear(xs, rwkv_w_rkv, j, F32, tm=2048, tn=1024, name="rwkv_rkv")
            y = rwkv_scan(rkv, a_gate, log_w, gate, rwkv_k_k[j], rwkv_k_a[j], rwkv_r_k[j].reshape(d),
                          rwkv_ln_g[j], rwkv_ln_b[j], batch, seq)
            w_out = rwkv_w_o
        else:
            y = shortconv_in(h_next, conv_w_in, j, conv_w[j], seq, tm=2048, tn=256)
            w_out = conv_w_out
        x, h = project_residual_norm(y, w_out, j, x, ffn_norm[i])
        act = ffn_gate_up(h, ffn_w_gu, i, ffn_conv_w[i], ffn_conv_b[i], seq, tm=2048)
        x = linear(act, ffn_w_down, i, F32, tm=512, tn=512, residual=x, name="ffn_down")
        if i + 1 == depth:
            _, out = ple_update(x, p, i, ple_w_gate, ple_w_proj, ple_norm[i], final_norm, F32)
            return out.reshape(batch, seq, d)
        if (i + 1) % 3 == 2:
            x, h_next = ple_update(x, p, i, ple_w_gate, ple_w_proj, ple_norm[i], conv_norm[(i + 1) // 3])
        else:
            x = ple_update(x, p, i, ple_w_gate, ple_w_proj, ple_norm[i])
```

```python
import functools

import jax
import jax.numpy as jnp
from jax import lax
from jax.experimental import pallas as pl
from jax.experimental.pallas import tpu as pltpu

F32 = jnp.float32
BF16 = jnp.bfloat16

NORM_EPS = 1e-6
NEG_INF = -1e30
RWKV_GN_EPS = 6.4e-4

ATTN_GROUPS = ((128, 1), (512, 4), (2048, 16))
ATTN_HEADS = 16
ATTN_HEAD_DIM = 128
ATTN_BLOCK = 128
ATTN_BLOCKS_IN_FLIGHT = 16
LOG2_E = 1.4426950408889634
EXP_MINUS_HALF = 0.6065306597126334
RWKV_HEAD_SIZE = 64
SCAN_CHUNK = 64
SCAN_CHUNKS_PER_STEP = 2
LANES = 128
CARRY_ROWS = 8
SUB_ROWS = 512
SUB_COLS = 512

VMEM_LIMIT_BYTES = 56 * 1024 * 1024


def _params(semantics):
    return pltpu.CompilerParams(dimension_semantics=semantics, vmem_limit_bytes=VMEM_LIMIT_BYTES)


def _sigmoid(x):
    return 1.0 / (1.0 + jnp.exp(-x))


def _rms(x, g):
    ms = jnp.mean(x * x, axis=-1, keepdims=True)
    return x * lax.rsqrt(ms + NORM_EPS) * g


def _shift_rows(cur, prev_tail, n):
    row = lax.broadcasted_iota(jnp.int32, cur.shape, 0)
    out = pltpu.roll(cur, n, axis=0)
    for r in range(n):
        src = CARRY_ROWS - n + r
        out = jnp.where(row == r, prev_tail[src:src + 1, :], out)
    return out


def _attn_norm_body(x_ref, g_ref, o0_ref, o1_ref, o2_ref, y_ref, *, tm):
    y = _rms(x_ref[...], g_ref[...])
    o0_ref[...] = y.astype(o0_ref.dtype)
    for c in range(y_ref.shape[0]):
        y_ref[c] = y[:, c * LANES:(c + 1) * LANES]
    for o_ref, dil in ((o1_ref, ATTN_GROUPS[1][1]), (o2_ref, ATTN_GROUPS[2][1])):
        for r in range(dil):
            for c in range(y_ref.shape[0]):
                o_ref[r, :, c * LANES:(c + 1) * LANES] = (
                    y_ref[c, pl.ds(r, tm // dil, stride=dil), :].astype(o_ref.dtype))


def attn_rmsnorm(x, g, batch, seq, tm=256):
    m, d = x.shape
    bps = seq // tm
    d1, d2 = ATTN_GROUPS[1][1], ATTN_GROUPS[2][1]
    outs = pl.pallas_call(
        functools.partial(_attn_norm_body, tm=tm),
        out_shape=(jax.ShapeDtypeStruct((m, d), BF16),
                   jax.ShapeDtypeStruct((batch, d1, seq // d1, d), BF16),
                   jax.ShapeDtypeStruct((batch, d2, seq // d2, d), BF16)),
        grid=(batch, bps),
        in_specs=[pl.BlockSpec((tm, d), lambda b, i: (b * bps + i, 0)),
                  pl.BlockSpec((1, d), lambda b, i: (0, 0))],
        out_specs=(pl.BlockSpec((tm, d), lambda b, i: (b * bps + i, 0)),
                   pl.BlockSpec((None, d1, tm // d1, d), lambda b, i: (b, 0, i, 0)),
                   pl.BlockSpec((None, d2, tm // d2, d), lambda b, i: (b, 0, i, 0))),
        scratch_shapes=[pltpu.VMEM((d // LANES, tm, LANES), F32)],
        compiler_params=_params(("arbitrary", "arbitrary")),
        name="attn_norm",
    )(x, g.reshape(1, d))
    return [o.reshape(m, d) for o in outs]


def _rwkv_prep_body(x_ref, g_ref, mu_ref, w0_ref, a0_ref, *refs, bps):
    lora_refs = refs[:6]
    rkv_ref, lw_ref, a_ref, gate_ref = refs[6:10]
    carry_ref = refs[10]
    lora_b = refs[11:17]

    @pl.when(pl.program_id(0) == 0)
    def _():
        for w_ref, wb_ref in zip(lora_refs, lora_b):
            _cast_weight(w_ref, wb_ref)

    @pl.when(pl.program_id(0) % bps == 0)
    def _():
        carry_ref[...] = jnp.zeros_like(carry_ref)

    h = _rms(x_ref[...], g_ref[...])
    xx = _shift_rows(h, carry_ref[...], 1) - h
    carry_ref[...] = h[h.shape[0] - CARRY_ROWS:, :]

    def mix(n):
        return (h + xx * mu_ref[n:n + 1, :]).astype(BF16)

    def low_rank(n, w1b_ref, w2b_ref, mid_fn):
        mid = mid_fn(jnp.dot(mix(n), w1b_ref[...], preferred_element_type=F32))
        return jnp.dot(mid.astype(BF16), w2b_ref[...], preferred_element_type=F32)

    for n in range(3):
        rkv_ref[n] = mix(n)
    u = w0_ref[...] + low_rank(3, lora_b[0], lora_b[1], jnp.tanh)
    lw_ref[...] = -EXP_MINUS_HALF * _sigmoid(u)
    a_ref[...] = _sigmoid(a0_ref[...] + low_rank(4, lora_b[2], lora_b[3], lambda t: t))
    gate_ref[...] = low_rank(5, lora_b[4], lora_b[5], _sigmoid)


def rwkv_prep(x, g, mu, w0, a0, loras, seq, tm=256):
    m, d = x.shape
    padded = []
    for idx in range(0, len(loras), 2):
        w1, w2 = loras[idx], loras[idx + 1]
        r = -(-w1.shape[1] // LANES) * LANES
        padded += [jnp.pad(w1, ((0, 0), (0, r - w1.shape[1]))), jnp.pad(w2, ((0, r - w2.shape[0]), (0, 0)))]
    row = pl.BlockSpec((tm, d), lambda i: (i, 0))
    vec = pl.BlockSpec((1, d), lambda i: (0, 0))
    whole = lambda a: pl.BlockSpec(a.shape, lambda i: (0,) * a.ndim, pipeline_mode=pl.Buffered(1))
    return pl.pallas_call(
        functools.partial(_rwkv_prep_body, bps=seq // tm),
        out_shape=(jax.ShapeDtypeStruct((3, m, d), BF16),) + (jax.ShapeDtypeStruct((m, d), F32),) * 3,
        grid=(m // tm,),
        in_specs=[row, vec, whole(mu), vec, vec] + [whole(w) for w in padded],
        out_specs=(pl.BlockSpec((3, tm, d), lambda i: (0, i, 0)), row, row, row),
        scratch_shapes=[pltpu.VMEM((CARRY_ROWS, d), F32)] + [pltpu.VMEM(w.shape, BF16) for w in padded],
        compiler_params=_params(("arbitrary",)),
        name="rwkv_prep",
    )(x, g.reshape(1, d), mu, w0.reshape(1, d), a0.reshape(1, d), *padded)


def _cast_weight(w_ref, wb_ref, rows=256):
    k = w_ref.shape[0]
    rows = min(rows, k)

    def chunk(c, carry):
        sl = pl.ds(pl.multiple_of(c * rows, rows), rows)
        wb_ref[sl, :] = w_ref[sl, :].astype(wb_ref.dtype)
        return carry

    lax.fori_loop(0, k // rows, chunk, 0)


def _mm_body(*refs, n_x, w_lhs, n_extra, n_scratch, epilogue, sub_rows):
    n_w = len(w_lhs)
    x_refs = refs[:n_x]
    w_refs = refs[n_x:n_x + n_w]
    e_refs = refs[n_x + n_w:n_x + n_w + n_extra]
    o_ref = refs[n_x + n_w + n_extra]
    wb_refs = refs[n_x + n_w + n_extra + 1:n_x + 2 * n_w + n_extra + 1]
    s_refs = refs[n_x + 2 * n_w + n_extra + 1:]
    assert len(s_refs) == n_scratch

    @pl.when(pl.program_id(2) == 0)
    def _():
        for w_ref, wb_ref in zip(w_refs, wb_refs):
            _cast_weight(w_ref, wb_ref)

    tm = o_ref.shape[-2]
    sub = min(sub_rows, tm)
    for c in range(tm // sub):
        rows = pl.ds(c * sub, sub)
        xs = [x_ref[rows, :].astype(BF16) for x_ref in x_refs]
        accs = [jnp.dot(xs[l], wb_ref[...], preferred_element_type=F32)
                for wb_ref, l in zip(wb_refs, w_lhs)]
        out = epilogue(accs, e_refs, s_refs, rows, c == 0).astype(o_ref.dtype)
        if len(o_ref.shape) == 2:
            o_ref[rows, :] = out
        else:
            for slab in range(o_ref.shape[0]):
                o_ref[slab, rows, :] = out[:, slab * LANES:(slab + 1) * LANES]


def matmul(xs, ws, extras, epilogue, out_shape, out_spec, grid, scratch=(), sub_rows=SUB_ROWS, name="matmul"):
    w_lhs = tuple(l for _, _, l in ws)
    wb_scratch = [pltpu.VMEM(spec.block_shape[-2:], BF16) for _, spec, _ in ws]
    body = functools.partial(_mm_body, n_x=len(xs), w_lhs=w_lhs, n_extra=len(extras),
                             n_scratch=len(scratch), epilogue=epilogue, sub_rows=sub_rows)
    return pl.pallas_call(
        body,
        out_shape=out_shape,
        grid=grid,
        in_specs=[s for _, s in xs] + [s for _, s, _ in ws] + [s for _, s in extras],
        out_specs=out_spec,
        scratch_shapes=wb_scratch + list(scratch),
        compiler_params=_params(("arbitrary", "arbitrary", "arbitrary")),
        name=name,
    )(*[a for a, _ in xs], *[a for a, _, _ in ws], *[a for a, _ in extras])


def _ep_plain(accs, e_refs, s_refs, rows, first_sub):
    return accs[0]


def _ep_residual(accs, e_refs, s_refs, rows, first_sub):
    return e_refs[0][rows, :] + accs[0]


def _ep_col_scale(accs, e_refs, s_refs, rows, first_sub):
    return accs[0] * e_refs[0][...]


def _conv3(cur, cw_ref, carry_ref, bps, first_sub):
    if first_sub:
        @pl.when(pl.program_id(2) % bps == 0)
        def _():
            carry_ref[...] = jnp.zeros_like(carry_ref)

    tail = carry_ref[...]
    d1 = _shift_rows(cur, tail, 1)
    d2 = _shift_rows(cur, tail, 2)
    carry_ref[...] = cur[cur.shape[0] - CARRY_ROWS:, :]
    w = cw_ref[...]
    return d2 * w[0:1, :] + d1 * w[1:2, :] + cur * w[2:3, :]


def _ep_ffn(accs, e_refs, s_refs, rows, first_sub, *, bps):
    gate = _conv3(accs[0], e_refs[0], s_refs[0], bps, first_sub) + e_refs[1][...]
    return gate * _sigmoid(gate) * accs[1]


def _ep_shortconv(accs, e_refs, s_refs, rows, first_sub, *, bps):
    return accs[0] * _conv3(accs[1] * accs[2], e_refs[0], s_refs[0], bps, first_sub)


def _layer_cols(w, layer, tn, first_block=0):
    k = w.shape[1]
    return pl.BlockSpec((None, k, tn), lambda g, j, i: (layer, 0, first_block + j))


def linear(x, w, layer, out_dtype, tm=512, tn=1024, n_out=None, first_col=0, residual=None, col_scale=None,
           slab_major=False, name="linear"):
    m, k = x.shape
    n = n_out or w.shape[2]
    tn = min(tn, n)
    extras = []
    epilogue = _ep_plain
    assert residual is None or col_scale is None
    if residual is not None:
        extras = [(residual, pl.BlockSpec((tm, tn), lambda g, j, i: (i, j)))]
        epilogue = _ep_residual
    if col_scale is not None:
        extras = [(col_scale, pl.BlockSpec((1, tn), lambda g, j, i: (0, j)))]
        epilogue = _ep_col_scale
    if slab_major:
        out_shape = jax.ShapeDtypeStruct((n // LANES, m, LANES), out_dtype)
        out_spec = pl.BlockSpec((tn // LANES, tm, LANES), lambda g, j, i: (j, i, 0))
    else:
        out_shape = jax.ShapeDtypeStruct((m, n), out_dtype)
        out_spec = pl.BlockSpec((tm, tn), lambda g, j, i: (i, j))
    return matmul(
        [(x, pl.BlockSpec((tm, k), lambda g, j, i: (i, 0)))],
        [(w, _layer_cols(w, layer, tn, first_col // tn), 0)],
        extras, epilogue, out_shape, out_spec,
        (1, n // tn, m // tm), name=name)


def grouped_linear(x, w, layer, out_dtype, tm=512, tn=1024, name="grouped_linear"):
    m, k = x.shape[1:]
    n_groups, _, n_out = w.shape[1:]
    tn = min(tn, n_out)
    return matmul(
        [(x, pl.BlockSpec((None, tm, k), lambda g, j, i: (g, i, 0)))],
        [(w, pl.BlockSpec((None, None, k, tn), lambda g, j, i: (layer, g, 0, j)), 0)],
        [], _ep_plain,
        jax.ShapeDtypeStruct((n_groups, m, n_out), out_dtype),
        pl.BlockSpec((None, tm, tn), lambda g, j, i: (g, i, j)),
        (n_groups, n_out // tn, m // tm), name=name)


def ffn_gate_up(h, w_gu, layer, conv_w, conv_b, seq, tm=512, tn=512):
    m, k = h.shape
    f = w_gu.shape[2] // 2
    nj = f // tn
    vec = lambda rows: pl.BlockSpec((rows, tn), lambda g, j, i: (0, j))
    return matmul(
        [(h, pl.BlockSpec((tm, k), lambda g, j, i: (i, 0)))],
        [(w_gu, _layer_cols(w_gu, layer, tn), 0),
         (w_gu, _layer_cols(w_gu, layer, tn, nj), 0)],
        [(conv_w, vec(conv_w.shape[0])), (conv_b.reshape(1, f), vec(1))],
        functools.partial(_ep_ffn, bps=seq // tm),
        jax.ShapeDtypeStruct((m, f), BF16),
        pl.BlockSpec((tm, tn), lambda g, j, i: (i, j)),
        (1, nj, m // tm),
        scratch=[pltpu.VMEM((CARRY_ROWS, tn), F32)], sub_rows=2 * SUB_ROWS, name="ffn_gate_up")


def shortconv_in(h, w_in, layer, conv_w, seq, tm=512, tn=512):
    m, k = h.shape
    d = w_in.shape[2] // 3
    nj = d // tn
    return matmul(
        [(h, pl.BlockSpec((tm, k), lambda g, j, i: (i, 0)))],
        [(w_in, _layer_cols(w_in, layer, tn, s * nj), 0) for s in range(3)],
        [(conv_w, pl.BlockSpec((conv_w.shape[0], tn), lambda g, j, i: (0, j)))],
        functools.partial(_ep_shortconv, bps=seq // tm),
        jax.ShapeDtypeStruct((m, d), BF16),
        pl.BlockSpec((tm, tn), lambda g, j, i: (i, j)),
        (1, nj, m // tm),
        scratch=[pltpu.VMEM((CARRY_ROWS, tn), F32)], name="shortconv_in")


def _whole_weight(w, layer):
    return pl.BlockSpec((None,) + w.shape[1:], lambda i: (layer, 0, 0), pipeline_mode=pl.Buffered(1))


def _norm_rows(o_ref, ssq, g_ref, h_ref):
    n = o_ref.shape[1]
    rstd = lax.rsqrt(ssq * (1.0 / n) + NORM_EPS)
    for c in range(n // SUB_COLS):
        cols = pl.ds(c * SUB_COLS, SUB_COLS)
        h_ref[:, cols] = (o_ref[:, cols] * rstd * g_ref[:, cols]).astype(h_ref.dtype)


def _proj_norm_body(y_ref, w_ref, res_ref, g_ref, o_ref, h_ref, wb_ref):
    @pl.when(pl.program_id(0) == 0)
    def _():
        _cast_weight(w_ref, wb_ref)

    if len(y_ref.shape) == 3:
        y = jnp.concatenate([y_ref[s] for s in range(y_ref.shape[0])], axis=1)
    else:
        y = y_ref[...]
    ssq = jnp.zeros((y.shape[0], 1), F32)
    for c in range(o_ref.shape[1] // SUB_COLS):
        cols = pl.ds(c * SUB_COLS, SUB_COLS)
        xn = res_ref[:, cols] + jnp.dot(y, wb_ref[:, cols], preferred_element_type=F32)
        o_ref[:, cols] = xn
        ssq = ssq + jnp.sum(xn * xn, axis=-1, keepdims=True)
    _norm_rows(o_ref, ssq, g_ref, h_ref)


def project_residual_norm(y, w, layer, residual, g, tm=512):
    k, n = w.shape[1:]
    m = residual.shape[0]
    row = lambda width: pl.BlockSpec((tm, width), lambda i: (i, 0))
    y_spec = row(k) if y.ndim == 2 else pl.BlockSpec((y.shape[0], tm, y.shape[2]), lambda i: (0, i, 0))
    return pl.pallas_call(
        _proj_norm_body,
        out_shape=(jax.ShapeDtypeStruct((m, n), F32), jax.ShapeDtypeStruct((m, n), BF16)),
        grid=(m // tm,),
        in_specs=[y_spec, _whole_weight(w, layer), row(n), pl.BlockSpec((1, n), lambda i: (0, 0))],
        out_specs=(row(n), row(n)),
        scratch_shapes=[pltpu.VMEM((k, n), BF16)],
        compiler_params=_params(("arbitrary",)),
        name="project_residual_norm",
    )(y, w, residual, g.reshape(1, n))


def _ple_body(x_ref, p_ref, wg_ref, wp_ref, g_ref, *refs, norm_next, keep_x):
    if norm_next and keep_x:
        gn_ref, o_ref, h_ref, wgb_ref, wpb_ref = refs
    elif norm_next:
        gn_ref, h_ref, wgb_ref, wpb_ref, o_ref = refs
    else:
        o_ref, wgb_ref, wpb_ref = refs

    @pl.when(pl.program_id(0) == 0)
    def _():
        _cast_weight(wg_ref, wgb_ref)
        _cast_weight(wp_ref, wpb_ref)

    hn = _rms(x_ref[...], g_ref[...]).astype(BF16)
    pb = p_ref[...].astype(BF16)
    ssq = jnp.zeros((hn.shape[0], 1), F32)
    for c in range(o_ref.shape[1] // SUB_COLS):
        cols = pl.ds(c * SUB_COLS, SUB_COLS)
        gate = jnp.dot(hn, wgb_ref[:, cols], preferred_element_type=F32)
        proj = jnp.dot(pb, wpb_ref[:, cols], preferred_element_type=F32)
        xn = x_ref[:, cols] + _sigmoid(gate) * proj
        o_ref[:, cols] = xn
        ssq = ssq + jnp.sum(xn * xn, axis=-1, keepdims=True)
    if norm_next:
        _norm_rows(o_ref, ssq, gn_ref, h_ref)


def ple_update(x, p, layer, w_gate, w_proj, g, g_next=None, next_dtype=BF16, keep_x=True, tm=512):
    m, d = x.shape
    kp = p.shape[2]
    norm_next = g_next is not None
    assert norm_next or keep_x
    row = pl.BlockSpec((tm, d), lambda i: (i, 0))
    vec = pl.BlockSpec((1, d), lambda i: (0, 0))
    operands = [x, p, w_gate, w_proj, g.reshape(1, d)] + ([g_next.reshape(1, d)] if norm_next else [])
    in_specs = [row, pl.BlockSpec((None, tm, kp), lambda i: (layer, i, 0)),
                _whole_weight(w_gate, layer), _whole_weight(w_proj, layer), vec, vec]
    out_shape = ([jax.ShapeDtypeStruct((m, d), F32)] if keep_x else []) + (
        [jax.ShapeDtypeStruct((m, d), next_dtype)] if norm_next else [])
    scratch = [pltpu.VMEM((d, d), BF16), pltpu.VMEM((kp, d), BF16)] + ([] if keep_x else [pltpu.VMEM((tm, d), F32)])
    return pl.pallas_call(
        functools.partial(_ple_body, norm_next=norm_next, keep_x=keep_x),
        out_shape=tuple(out_shape),
        grid=(m // tm,),
        in_specs=in_specs[:len(operands)],
        out_specs=(row,) * len(out_shape),
        scratch_shapes=scratch,
        compiler_params=_params(("arbitrary",)),
        name="ple_update",
    )(*operands)


def _attn_body(slopes_ref, *refs):
    qkv_refs = refs[:9]
    o_ref = refs[9]
    out_ref, lse_ref = refs[10:]
    head = pl.program_id(1)
    blk = ATTN_BLOCK

    qi = lax.broadcasted_iota(jnp.int32, (blk, 2 * blk), 0)
    kj = lax.broadcasted_iota(jnp.int32, (blk, 2 * blk), 1)
    dist = qi + blk - kj
    distf = dist.astype(F32)

    blocks = []
    for g, (window, dil) in enumerate(ATTN_GROUPS):
        seq = qkv_refs[3 * g].shape[0]
        nb = seq // dil // blk
        n_back = window // dil
        coef = slopes_ref[g, head] * (float(dil) * LOG2_E)
        bias2 = jnp.where((dist >= 0) & (dist <= n_back), -(coef * distf), NEG_INF)
        bias1 = bias2[:, blk:]
        for j in range(seq // blk):
            blocks.append((g, dil, j // nb, j % nb, j, bias1, bias2))

    def combine(first_row, n_rows):
        rows_per_step = 256
        for c in range(n_rows // rows_per_step):
            sl = pl.ds(first_row + c * rows_per_step, rows_per_step)
            lses = [lse_ref[g, sl, :] for g in range(3)]
            top = jnp.maximum(jnp.maximum(lses[0], lses[1]), lses[2])
            es = [jnp.exp2(x - top) for x in lses]
            num = es[0] * out_ref[0, sl, :] + es[1] * out_ref[1, sl, :] + es[2] * out_ref[2, sl, :]
            o_ref[sl, :] = (num / (es[0] + es[1] + es[2])).astype(o_ref.dtype)

    stage_rows = blk * ATTN_GROUPS[1][1]
    n_stages = o_ref.shape[0] // stage_rows
    wide = [b for b in blocks if blk * b[1] > stage_rows]
    batches = [(wide[b0:b0 + ATTN_BLOCKS_IN_FLIGHT], None) for b0 in range(0, len(wide), ATTN_BLOCKS_IN_FLIGHT)]
    for st in range(n_stages):
        batches.append(([b for b in blocks if blk * b[1] <= stage_rows and (b[3] * blk * b[1]) // stage_rows == st],
                        st))

    for batch, stage in batches:
        s_list, v_list = [], []
        for g, dil, r, n, j, bias1, bias2 in batch:
            q_ref, k_ref, v_ref = qkv_refs[3 * g:3 * g + 3]
            lo = j * blk if n == 0 else (j - 1) * blk
            s = lax.dot_general(q_ref[j * blk:(j + 1) * blk, :], k_ref[lo:(j + 1) * blk, :],
                                (((1,), (1,)), ((), ())), preferred_element_type=F32)
            s_list.append(s + (bias1 if n == 0 else bias2))
            v_list.append(v_ref[lo:(j + 1) * blk, :])
        m_list = [jnp.max(s, axis=1, keepdims=True) for s in s_list]
        p_list = [jnp.exp2(s - m) for s, m in zip(s_list, m_list)]
        l_list = [jnp.sum(p, axis=1, keepdims=True) for p in p_list]
        acc_list = [jnp.dot(p.astype(BF16), v, preferred_element_type=F32) for p, v in zip(p_list, v_list)]
        for (g, dil, r, n, j, _, _), m, l, acc in zip(batch, m_list, l_list, acc_list):
            start = n * blk * dil + r
            rows = pl.ds(start, blk) if dil == 1 else pl.ds(start, blk, stride=dil)
            out_ref[g, rows, :] = acc * (1.0 / l)
            lse_ref[g, rows, :] = jnp.broadcast_to(m + jnp.log2(l), acc.shape)
        if stage is not None:
            combine(stage * stage_rows, stage_rows)


def dilated_attention(qkv, slopes, batch, seq):
    n_g = len(qkv)
    m = qkv[0].shape[1]
    hd = ATTN_HEAD_DIM
    in_specs = [pl.BlockSpec(memory_space=pltpu.SMEM)]
    for g in range(n_g):
        for c in range(3):
            in_specs.append(pl.BlockSpec((None, seq, hd), lambda b, h, c=c: (c * ATTN_HEADS + h, b, 0)))
    return pl.pallas_call(
        _attn_body,
        out_shape=jax.ShapeDtypeStruct((ATTN_HEADS, m, hd), BF16),
        grid=(batch, ATTN_HEADS),
        in_specs=in_specs,
        out_specs=pl.BlockSpec((None, seq, hd), lambda b, h: (h, b, 0)),
        scratch_shapes=[pltpu.VMEM((n_g, seq, hd), F32)] * 2,
        compiler_params=_params(("arbitrary", "arbitrary")),
        name="dilated_attention",
    )(slopes, *[a for a in qkv for _ in range(3)])


def _split3(x):
    hi = x.astype(BF16)
    r1 = x - hi.astype(F32)
    mid = r1.astype(BF16)
    lo = (r1 - mid.astype(F32)).astype(BF16)
    return hi, mid, lo


def _dot_split2_rhs(x, rhs_bf16):
    hi = x.astype(BF16)
    lo = (x - hi.astype(F32)).astype(BF16)
    return jnp.dot(hi, rhs_bf16, preferred_element_type=F32) + jnp.dot(lo, rhs_bf16, preferred_element_type=F32)


def _dot_exact_lhs(lhs_bf16, x):
    return sum(jnp.dot(lhs_bf16, part, preferred_element_type=F32) for part in _split3(x))


def _dot_t(a, b):
    return lax.dot_general(a, b, (((1,), (1,)), ((), ())), preferred_element_type=F32)


def _t_dot(a, b):
    return lax.dot_general(a, b, (((0,), (0,)), ((), ())), preferred_element_type=F32)


def _scan_chunk_body(r_ref, k_ref, v_ref, a_ref, lw_ref, g_ref, kk_ref, ka_ref, rk_ref, lng_ref, lnb_ref,
                     o_ref, state_ref, lhs_ref, rhs_ref, vs_ref, tail_ref):
    t = SCAN_CHUNK
    n_sub = r_ref.shape[0] // t
    n_pairs = r_ref.shape[1] // LANES
    pairs = range(n_pairs)

    @pl.when(pl.program_id(1) == 0)
    def _():
        state_ref[...] = jnp.zeros_like(state_ref)

    lane = lax.broadcasted_iota(jnp.int32, (t, LANES), 1)
    head0 = lane < RWKV_HEAD_SIZE
    ri = lax.broadcasted_iota(jnp.int32, (2 * t, 2 * t), 0)
    ci = lax.broadcasted_iota(jnp.int32, (2 * t, 2 * t), 1)
    same_head = (ri < t) == (ci < t)
    strict = same_head & (ci < ri)
    incl = same_head & (ci <= ri)
    eye = jnp.where(ri == ci, 1.0, 0.0)
    tri = jnp.where(lax.broadcasted_iota(jnp.int32, (t, t), 1) <= lax.broadcasted_iota(jnp.int32, (t, t), 0),
                    1.0, 0.0).astype(BF16)
    li = lax.broadcasted_iota(jnp.int32, (LANES, LANES), 0)
    lj = lax.broadcasted_iota(jnp.int32, (LANES, LANES), 1)
    head_ones = jnp.where((li < RWKV_HEAD_SIZE) == (lj < RWKV_HEAD_SIZE), 1.0, 0.0).astype(BF16)
    zeros = jnp.zeros_like(head_ones)
    head_ones2 = jnp.concatenate([jnp.concatenate([head_ones, zeros], axis=1),
                                  jnp.concatenate([zeros, head_ones], axis=1)], axis=0)
    n_doublings = (t - 1).bit_length()

    def lanes(x, p):
        return x[:, p * LANES:(p + 1) * LANES]

    def stack(x):
        return jnp.concatenate([jnp.where(head0, x, 0.0), jnp.where(head0, 0.0, x)], axis=0)

    def tall(x):
        return jnp.concatenate([lanes(x, p) for p in pairs], axis=0)

    def wide(s):
        return jnp.concatenate([s[p * t:(p + 1) * t, :] for p in pairs], axis=1)

    def head_sum(x):
        return wide(_dot_split2_rhs(tall(x), head_ones))

    def head_sum_two(x1, x2):
        s = _dot_split2_rhs(jnp.concatenate([tall(x1), tall(x2)], axis=1), head_ones2)
        return wide(s[:, :LANES]), wide(s[:, LANES:])

    def mm(a, b):
        return jnp.dot(a, b, preferred_element_type=F32)

    inv_n = 1.0 / RWKV_HEAD_SIZE

    def operands(c):
        rows = pl.ds(c * t, t)
        r = r_ref[rows, :]
        k0 = k_ref[rows, :]
        a_gate = a_ref[rows, :]
        lw = lw_ref[rows, :]
        kk = k0 * kk_ref[...]
        k = k0 * (1.0 + (a_gate - 1.0) * ka_ref[...])
        kk_sq, rk_sum = head_sum_two(kk * kk, r * k * rk_ref[...])
        kk = kk / jnp.maximum(jnp.sqrt(kk_sq), 1e-12)
        b = kk * a_gate
        lg = _dot_exact_lhs(tri, lw)
        g_incl = jnp.exp(lg)
        g_inv = jnp.exp(-lg)
        g_last = g_incl[t - 1:t, :]
        g_tail = g_last * g_inv
        at = -kk * jnp.exp(lg - lw)
        rt = r * g_incl
        bt = b * g_inv
        kt = k * g_inv
        bh = b * g_tail
        kh = k * g_tail
        for p in pairs:
            lhs_ref[c, p, :2 * t, :] = stack(lanes(at, p)).astype(BF16)
            lhs_ref[c, p, 2 * t:, :] = stack(lanes(rt, p)).astype(BF16)
            btp = lanes(bt, p).astype(BF16)
            ktp = lanes(kt, p).astype(BF16)
            rhs_ref[c, p] = jnp.concatenate([btp, btp, ktp, ktp], axis=0)
            vs_ref[c, p] = stack(lanes(v_ref[rows, :], p)).astype(BF16)
            tail_ref[c, p, :2 * t, :] = stack(lanes(bh, p)).astype(BF16)
            tail_ref[c, p, 2 * t:, :] = stack(lanes(kh, p)).astype(BF16)
        return g_last, rk_sum

    def triangular(c):
        n_ab, n_ak, b_r = [], [], []
        for p in pairs:
            sc = _dot_t(lhs_ref[c, p], rhs_ref[c, p])
            n_ab.append(jnp.where(strict, sc[:2 * t, :2 * t], 0.0))
            n_ak.append(jnp.where(strict, sc[:2 * t, 2 * t:], 0.0).astype(BF16))
            b_r.append(jnp.concatenate([jnp.where(incl, sc[2 * t:, :2 * t], 0.0),
                                        jnp.where(incl, sc[2 * t:, 2 * t:], 0.0)], axis=1).astype(BF16))
        inv = [eye + n for n in n_ab]
        pw = n_ab
        for _ in range(n_doublings - 1):
            pwb = [x.astype(BF16) for x in pw]
            pw = [mm(x, x) for x in pwb]
            inv = [i_p + mm(q.astype(BF16), i_p.astype(BF16)) for i_p, q in zip(inv, pw)]
        return n_ak, b_r, [x.astype(BF16) for x in inv]

    def advance(c, g_last, n_ak, b_r, inv):
        state = [state_ref[p] for p in pairs]
        state_b = [s.astype(BF16) for s in state]
        w = [_dot_t(lhs_ref[c, p, :2 * t, :], state_b[p]) + mm(n_ak[p], vs_ref[c, p]) for p in pairs]
        zv = [jnp.concatenate([mm(inv[p], w[p].astype(BF16)).astype(BF16), vs_ref[c, p]], axis=0) for p in pairs]
        y_parts = []
        for p in pairs:
            y_s = _dot_t(lhs_ref[c, p, 2 * t:, :], state_b[p]) + mm(b_r[p], zv[p])
            y_parts.append(y_s[:t, :] + y_s[t:, :])
            state_ref[p] = state[p] * lanes(g_last, p) + _t_dot(zv[p], tail_ref[c, p])
        return jnp.concatenate(y_parts, axis=1)

    def finish(c, y, y_sum, rk_sum):
        rows = pl.ds(c * t, t)
        yc = y - y_sum * inv_n
        var = head_sum(yc * yc) * inv_n
        yn = yc * lax.rsqrt(var + RWKV_GN_EPS) * lng_ref[...] + lnb_ref[...]
        o_ref[rows, :] = ((yn + rk_sum * v_ref[rows, :]) * g_ref[rows, :]).astype(o_ref.dtype)

    prepared = [operands(c) for c in range(n_sub)]
    solved = [triangular(c) for c in range(n_sub)]
    pending = None
    for c in range(n_sub):
        y = advance(c, prepared[c][0], *solved[c])
        if pending is not None:
            finish(*pending)
        pending = (c, y, head_sum(y), prepared[c][1])
    finish(*pending)


def rwkv_scan(rkv, a_gate, log_w, gate, k_k, k_a, r_k, ln_g, ln_b, batch, seq):
    _, m, d = rkv.shape
    t = SCAN_CHUNK
    n_sub = SCAN_CHUNKS_PER_STEP
    nc = seq // (n_sub * t)
    n_pairs = d // LANES
    row = lambda g: pl.BlockSpec((None, n_sub * t, d), lambda b, c, g=g: (g, b * nc + c, 0))
    tile = pl.BlockSpec((n_sub * t, d), lambda b, c: (b * nc + c, 0))
    vec = pl.BlockSpec((1, d), lambda b, c: (0, 0))
    return pl.pallas_call(
        _scan_chunk_body,
        out_shape=jax.ShapeDtypeStruct((m, d), BF16),
        grid=(batch, nc),
        in_specs=[row(0), row(1), row(2), tile, tile, tile, vec, vec, vec, vec, vec],
        out_specs=tile,
        scratch_shapes=[pltpu.VMEM((n_pairs, LANES, LANES), F32),
                        pltpu.VMEM((n_sub, n_pairs, 4 * t, LANES), BF16),
                        pltpu.VMEM((n_sub, n_pairs, 4 * t, LANES), BF16),
                        pltpu.VMEM((n_sub, n_pairs, 2 * t, LANES), BF16),
                        pltpu.VMEM((n_sub, n_pairs, 4 * t, LANES), BF16)],
        compiler_params=_params(("arbitrary", "arbitrary")),
        name="rwkv_scan",
    )(rkv, rkv, rkv, a_gate, log_w, gate,
      k_k.reshape(1, d), k_a.reshape(1, d), r_k.reshape(1, d), ln_g.reshape(1, d), ln_b.reshape(1, d))


def _alibi_slopes(n_groups, n_heads):
    n = n_groups * n_heads
    idx = jnp.arange(1, n + 1, dtype=F32)
    return (2.0 ** (-8.0 * idx / n)).reshape(n_groups, n_heads)


def kernel(x, p, attn_norm, attn_w_qkv, attn_w_o, rwkv_norm, rwkv_mu, rwkv_w_rkv, rwkv_w0, rwkv_w_w1, rwkv_w_w2, rwkv_a0, rwkv_w_a1, rwkv_w_a2, rwkv_w_g1, rwkv_w_g2, rwkv_k_k, rwkv_k_a, rwkv_r_k, rwkv_ln_g, rwkv_ln_b, rwkv_w_o, conv_norm, conv_w_in, conv_w, conv_w_out, ffn_norm, ffn_w_gu, ffn_conv_w, ffn_conv_b, ffn_w_down, ple_w_proj, ple_norm, ple_w_gate, final_norm):
    batch, seq, d = x.shape
    depth = p.shape[0]
    m = batch * seq
    n_groups = len(ATTN_GROUPS)
    slopes = _alibi_slopes(n_groups, ATTN_HEADS)
    x = x.reshape(m, d)
    p = p.reshape(depth, m, p.shape[-1])

    h_next = None
    for i in range(depth):
        kind, j = i % 3, i // 3
        if kind == 0:
            hs = attn_rmsnorm(x, attn_norm[j], batch, seq)
            n_q = ATTN_HEADS * ATTN_HEAD_DIM
            q_scale = jnp.concatenate([jnp.full((1, n_q), ATTN_HEAD_DIM ** -0.5 * LOG2_E, F32),
                                       jnp.ones((1, 2 * n_q), F32)], axis=1)
            qkv = [linear(hs[g], attn_w_qkv, j, BF16, tm=2048, n_out=3 * n_q, first_col=g * 3 * n_q,
                          col_scale=q_scale, slab_major=True, name="attn_qkv") for g in range(n_groups)]
            y = dilated_attention(qkv, slopes, batch, seq)
            w_out = attn_w_o
        elif kind == 1:
            xs, log_w, a_gate, gate = rwkv_prep(
                x, rwkv_norm[j], rwkv_mu[j], rwkv_w0[j], rwkv_a0[j],
                [rwkv_w_w1[j], rwkv_w_w2[j], rwkv_w_a1[j], rwkv_w_a2[j], rwkv_w_g1[j], rwkv_w_g2[j]], seq)
            rkv = grouped_linear(xs, rwkv_w_rkv, j, F32, tm=2048, tn=1024, name="rwkv_rkv")
            y = rwkv_scan(rkv, a_gate, log_w, gate, rwkv_k_k[j], rwkv_k_a[j], rwkv_r_k[j].reshape(d),
                          rwkv_ln_g[j], rwkv_ln_b[j], batch, seq)
            w_out = rwkv_w_o
        else:
            y = shortconv_in(h_next, conv_w_in, j, conv_w[j], seq, tm=2048, tn=256)
            w_out = conv_w_out
        x, h = project_residual_norm(y, w_out, j, x, ffn_norm[i])
        act = ffn_gate_up(h, ffn_w_gu, i, ffn_conv_w[i], ffn_conv_b[i], seq, tm=2048)
        x = linear(act, ffn_w_down, i, F32, tm=512, tn=512, residual=x, name="ffn_down")
        if i + 1 == depth:
            out, = ple_update(x, p, i, ple_w_gate, ple_w_proj, ple_norm[i], final_norm, F32, keep_x=False)
            return out.reshape(batch, seq, d)
        if (i + 1) % 3 == 2:
            x, h_next = ple_update(x, p, i, ple_w_gate, ple_w_proj, ple_norm[i], conv_norm[(i + 1) // 3])
        else:
            x, = ple_update(x, p, i, ple_w_gate, ple_w_proj, ple_norm[i])
```

```python
import functools

import jax
import jax.numpy as jnp
from jax import lax
from jax.experimental import pallas as pl
from jax.experimental.pallas import tpu as pltpu

F32 = jnp.float32
BF16 = jnp.bfloat16

NORM_EPS = 1e-6
NEG_INF = -1e30
RWKV_GN_EPS = 6.4e-4

ATTN_GROUPS = ((128, 1), (512, 4), (2048, 16))
ATTN_HEADS = 16
ATTN_HEAD_DIM = 128
ATTN_BLOCK = 128
ATTN_BLOCKS_IN_FLIGHT = 16
LOG2_E = 1.4426950408889634
EXP_MINUS_HALF = 0.6065306597126334
RWKV_HEAD_SIZE = 64
SCAN_CHUNK = 64
SCAN_CHUNKS_PER_STEP = 2
LANES = 128
CARRY_ROWS = 8
SUB_ROWS = 512
SUB_COLS = 512

VMEM_LIMIT_BYTES = 56 * 1024 * 1024


def _params(semantics):
    return pltpu.CompilerParams(dimension_semantics=semantics, vmem_limit_bytes=VMEM_LIMIT_BYTES)


def _sigmoid(x):
    return 1.0 / (1.0 + jnp.exp(-x))


def _rms(x, g):
    ms = jnp.mean(x * x, axis=-1, keepdims=True)
    return x * lax.rsqrt(ms + NORM_EPS) * g


def _shift_rows(cur, prev_tail, n):
    row = lax.broadcasted_iota(jnp.int32, cur.shape, 0)
    out = pltpu.roll(cur, n, axis=0)
    for r in range(n):
        src = CARRY_ROWS - n + r
        out = jnp.where(row == r, prev_tail[src:src + 1, :], out)
    return out


def _attn_norm_body(x_ref, g_ref, o0_ref, o1_ref, o2_ref, y_ref, *, tm):
    y = _rms(x_ref[...], g_ref[...])
    o0_ref[...] = y.astype(o0_ref.dtype)
    for c in range(y_ref.shape[0]):
        y_ref[c] = y[:, c * LANES:(c + 1) * LANES]
    for o_ref, dil in ((o1_ref, ATTN_GROUPS[1][1]), (o2_ref, ATTN_GROUPS[2][1])):
        for r in range(dil):
            for c in range(y_ref.shape[0]):
                o_ref[r, :, c * LANES:(c + 1) * LANES] = (
                    y_ref[c, pl.ds(r, tm // dil, stride=dil), :].astype(o_ref.dtype))


def attn_rmsnorm(x, g, batch, seq, tm=256):
    m, d = x.shape
    bps = seq // tm
    d1, d2 = ATTN_GROUPS[1][1], ATTN_GROUPS[2][1]
    outs = pl.pallas_call(
        functools.partial(_attn_norm_body, tm=tm),
        out_shape=(jax.ShapeDtypeStruct((m, d), BF16),
                   jax.ShapeDtypeStruct((batch, d1, seq // d1, d), BF16),
                   jax.ShapeDtypeStruct((batch, d2, seq // d2, d), BF16)),
        grid=(batch, bps),
        in_specs=[pl.BlockSpec((tm, d), lambda b, i: (b * bps + i, 0)),
                  pl.BlockSpec((1, d), lambda b, i: (0, 0))],
        out_specs=(pl.BlockSpec((tm, d), lambda b, i: (b * bps + i, 0)),
                   pl.BlockSpec((None, d1, tm // d1, d), lambda b, i: (b, 0, i, 0)),
                   pl.BlockSpec((None, d2, tm // d2, d), lambda b, i: (b, 0, i, 0))),
        scratch_shapes=[pltpu.VMEM((d // LANES, tm, LANES), F32)],
        compiler_params=_params(("arbitrary", "arbitrary")),
        name="attn_norm",
    )(x, g.reshape(1, d))
    return [o.reshape(m, d) for o in outs]


def _rwkv_prep_body(x_ref, g_ref, mu_ref, w0_ref, a0_ref, *refs, bps):
    lora_refs = refs[:6]
    rkv_ref, lw_ref, a_ref, gate_ref = refs[6:10]
    carry_ref = refs[10]
    lora_b = refs[11:17]

    @pl.when(pl.program_id(0) == 0)
    def _():
        for w_ref, wb_ref in zip(lora_refs, lora_b):
            _cast_weight(w_ref, wb_ref)

    @pl.when(pl.program_id(0) % bps == 0)
    def _():
        carry_ref[...] = jnp.zeros_like(carry_ref)

    h = _rms(x_ref[...], g_ref[...])
    xx = _shift_rows(h, carry_ref[...], 1) - h
    carry_ref[...] = h[h.shape[0] - CARRY_ROWS:, :]

    def mix(n):
        return (h + xx * mu_ref[n:n + 1, :]).astype(BF16)

    def low_rank(n, w1b_ref, w2b_ref, mid_fn):
        mid = mid_fn(jnp.dot(mix(n), w1b_ref[...], preferred_element_type=F32))
        return jnp.dot(mid.astype(BF16), w2b_ref[...], preferred_element_type=F32)

    for n in range(3):
        rkv_ref[n] = mix(n)
    u = w0_ref[...] + low_rank(3, lora_b[0], lora_b[1], jnp.tanh)
    lw_ref[...] = -EXP_MINUS_HALF * _sigmoid(u)
    a_ref[...] = _sigmoid(a0_ref[...] + low_rank(4, lora_b[2], lora_b[3], lambda t: t))
    gate_ref[...] = low_rank(5, lora_b[4], lora_b[5], _sigmoid)


def rwkv_prep(x, g, mu, w0, a0, loras, seq, tm=256):
    m, d = x.shape
    padded = []
    for idx in range(0, len(loras), 2):
        w1, w2 = loras[idx], loras[idx + 1]
        r = -(-w1.shape[1] // LANES) * LANES
        padded += [jnp.pad(w1, ((0, 0), (0, r - w1.shape[1]))), jnp.pad(w2, ((0, r - w2.shape[0]), (0, 0)))]
    row = pl.BlockSpec((tm, d), lambda i: (i, 0))
    vec = pl.BlockSpec((1, d), lambda i: (0, 0))
    whole = lambda a: pl.BlockSpec(a.shape, lambda i: (0,) * a.ndim, pipeline_mode=pl.Buffered(1))
    return pl.pallas_call(
        functools.partial(_rwkv_prep_body, bps=seq // tm),
        out_shape=(jax.ShapeDtypeStruct((3, m, d), BF16),) + (jax.ShapeDtypeStruct((m, d), F32),) * 3,
        grid=(m // tm,),
        in_specs=[row, vec, whole(mu), vec, vec] + [whole(w) for w in padded],
        out_specs=(pl.BlockSpec((3, tm, d), lambda i: (0, i, 0)), row, row, row),
        scratch_shapes=[pltpu.VMEM((CARRY_ROWS, d), F32)] + [pltpu.VMEM(w.shape, BF16) for w in padded],
        compiler_params=_params(("arbitrary",)),
        name="rwkv_prep",
    )(x, g.reshape(1, d), mu, w0.reshape(1, d), a0.reshape(1, d), *padded)


def _cast_weight(w_ref, wb_ref, rows=256):
    k = w_ref.shape[0]
    rows = min(rows, k)

    def chunk(c, carry):
        sl = pl.ds(pl.multiple_of(c * rows, rows), rows)
        wb_ref[sl, :] = w_ref[sl, :].astype(wb_ref.dtype)
        return carry

    lax.fori_loop(0, k // rows, chunk, 0)


def _mm_body(*refs, n_x, w_lhs, w_is_f32, n_extra, n_scratch, epilogue, sub_rows, side_cast):
    refs = list(refs)
    x_refs = [refs.pop(0) for _ in range(n_x)]
    w_refs = [refs.pop(0) for _ in w_lhs]
    e_refs = [refs.pop(0) for _ in range(n_extra)]
    side_in = refs.pop(0) if side_cast else None
    o_ref = refs.pop(0)
    side_out = refs.pop(0) if side_cast else None
    cast_refs = [refs.pop(0) for f in w_is_f32 if f]
    s_refs = refs
    assert len(s_refs) == n_scratch
    cast_iter = iter(cast_refs)
    wb_refs = [next(cast_iter) if f else w_ref for w_ref, f in zip(w_refs, w_is_f32)]

    @pl.when(pl.program_id(2) == 0)
    def _():
        for w_ref, wb_ref, f in zip(w_refs, wb_refs, w_is_f32):
            if f:
                _cast_weight(w_ref, wb_ref)

    if side_cast:
        side_out[...] = side_in[...].astype(side_out.dtype)

    tm = o_ref.shape[-2]
    sub = min(sub_rows, tm)
    for c in range(tm // sub):
        rows = pl.ds(c * sub, sub)
        xs = [x_ref[rows, :].astype(BF16) for x_ref in x_refs]
        accs = [jnp.dot(xs[l], wb_ref[...], preferred_element_type=F32)
                for wb_ref, l in zip(wb_refs, w_lhs)]
        out = epilogue(accs, e_refs, s_refs, rows, c == 0).astype(o_ref.dtype)
        if len(o_ref.shape) == 2:
            o_ref[rows, :] = out
        else:
            for slab in range(o_ref.shape[0]):
                o_ref[slab, rows, :] = out[:, slab * LANES:(slab + 1) * LANES]


def matmul(xs, ws, extras, epilogue, out_shape, out_spec, grid, scratch=(), sub_rows=SUB_ROWS, side_cast=None,
           name="matmul"):
    w_lhs = tuple(l for _, _, l in ws)
    w_is_f32 = tuple(a.dtype == F32 for a, _, _ in ws)
    wb_scratch = [pltpu.VMEM(spec.block_shape[-2:], BF16) for (_, spec, _), f in zip(ws, w_is_f32) if f]
    body = functools.partial(_mm_body, n_x=len(xs), w_lhs=w_lhs, w_is_f32=w_is_f32, n_extra=len(extras),
                             n_scratch=len(scratch), epilogue=epilogue, sub_rows=sub_rows,
                             side_cast=side_cast is not None)
    in_specs = [s for _, s in xs] + [s for _, s, _ in ws] + [s for _, s in extras]
    operands = [a for a, _ in xs] + [a for a, _, _ in ws] + [a for a, _ in extras]
    if side_cast is not None:
        in_specs.append(side_cast[1])
        operands.append(side_cast[0])
        out_shape, out_spec = (out_shape, side_cast[2]), (out_spec, side_cast[3])
    return pl.pallas_call(
        body,
        out_shape=out_shape,
        grid=grid,
        in_specs=in_specs,
        out_specs=out_spec,
        scratch_shapes=wb_scratch + list(scratch),
        compiler_params=_params(("arbitrary", "arbitrary", "arbitrary")),
        name=name,
    )(*operands)


def _ep_plain(accs, e_refs, s_refs, rows, first_sub):
    return accs[0]


def _ep_residual(accs, e_refs, s_refs, rows, first_sub):
    return e_refs[0][rows, :] + accs[0]


def _ep_col_scale(accs, e_refs, s_refs, rows, first_sub):
    return accs[0] * e_refs[0][...]


def _conv3(cur, cw_ref, carry_ref, bps, first_sub):
    if first_sub:
        @pl.when(pl.program_id(2) % bps == 0)
        def _():
            carry_ref[...] = jnp.zeros_like(carry_ref)

    tail = carry_ref[...]
    d1 = _shift_rows(cur, tail, 1)
    d2 = _shift_rows(cur, tail, 2)
    carry_ref[...] = cur[cur.shape[0] - CARRY_ROWS:, :]
    w = cw_ref[...]
    return d2 * w[0:1, :] + d1 * w[1:2, :] + cur * w[2:3, :]


def _ep_ffn(accs, e_refs, s_refs, rows, first_sub, *, bps):
    gate = _conv3(accs[0], e_refs[0], s_refs[0], bps, first_sub) + e_refs[1][...]
    return gate * _sigmoid(gate) * accs[1]


def _ep_shortconv(accs, e_refs, s_refs, rows, first_sub, *, bps):
    return accs[0] * _conv3(accs[1] * accs[2], e_refs[0], s_refs[0], bps, first_sub)


def _layer_cols(w, layer, tn, first_block=0):
    if layer is None:
        return pl.BlockSpec((w.shape[0], tn), lambda g, j, i: (0, first_block + j))
    return pl.BlockSpec((None, w.shape[1], tn), lambda g, j, i: (layer, 0, first_block + j))


def linear(x, w, layer, out_dtype, tm=512, tn=1024, n_out=None, first_col=0, residual=None, col_scale=None,
           slab_major=False, name="linear"):
    m, k = x.shape
    n = n_out or w.shape[-1]
    tn = min(tn, n)
    extras = []
    epilogue = _ep_plain
    assert residual is None or col_scale is None
    if residual is not None:
        extras = [(residual, pl.BlockSpec((tm, tn), lambda g, j, i: (i, j)))]
        epilogue = _ep_residual
    if col_scale is not None:
        extras = [(col_scale, pl.BlockSpec((1, tn), lambda g, j, i: (0, j)))]
        epilogue = _ep_col_scale
    if slab_major:
        out_shape = jax.ShapeDtypeStruct((n // LANES, m, LANES), out_dtype)
        out_spec = pl.BlockSpec((tn // LANES, tm, LANES), lambda g, j, i: (j, i, 0))
    else:
        out_shape = jax.ShapeDtypeStruct((m, n), out_dtype)
        out_spec = pl.BlockSpec((tm, tn), lambda g, j, i: (i, j))
    return matmul(
        [(x, pl.BlockSpec((tm, k), lambda g, j, i: (i, 0)))],
        [(w, _layer_cols(w, layer, tn, first_col // tn), 0)],
        extras, epilogue, out_shape, out_spec,
        (1, n // tn, m // tm), name=name)


def grouped_linear(x, w, layer, out_dtype, tm=512, tn=1024, name="grouped_linear"):
    m, k = x.shape[1:]
    n_groups, _, n_out = w.shape[1:]
    tn = min(tn, n_out)
    return matmul(
        [(x, pl.BlockSpec((None, tm, k), lambda g, j, i: (g, i, 0)))],
        [(w, pl.BlockSpec((None, None, k, tn), lambda g, j, i: (layer, g, 0, j)), 0)],
        [], _ep_plain,
        jax.ShapeDtypeStruct((n_groups, m, n_out), out_dtype),
        pl.BlockSpec((None, tm, tn), lambda g, j, i: (g, i, j)),
        (n_groups, n_out // tn, m // tm), name=name)


def ffn_gate_up(h, w_gu, layer, conv_w, conv_b, seq, w_down, tm=512, tn=512):
    m, k = h.shape
    f = w_gu.shape[2] // 2
    nj = f // tn
    ni = m // tm
    vec = lambda rows: pl.BlockSpec((rows, tn), lambda g, j, i: (0, j))
    kd, nd = w_down.shape[1:]
    rows_per_step = kd // (nj * ni)
    assert rows_per_step * nj * ni == kd and rows_per_step % 16 == 0
    side_cast = (w_down, pl.BlockSpec((None, rows_per_step, nd), lambda g, j, i: (layer, j * ni + i, 0)),
                 jax.ShapeDtypeStruct((kd, nd), BF16),
                 pl.BlockSpec((rows_per_step, nd), lambda g, j, i: (j * ni + i, 0)))
    return matmul(
        [(h, pl.BlockSpec((tm, k), lambda g, j, i: (i, 0)))],
        [(w_gu, _layer_cols(w_gu, layer, tn), 0),
         (w_gu, _layer_cols(w_gu, layer, tn, nj), 0)],
        [(conv_w, vec(conv_w.shape[0])), (conv_b.reshape(1, f), vec(1))],
        functools.partial(_ep_ffn, bps=seq // tm),
        jax.ShapeDtypeStruct((m, f), BF16),
        pl.BlockSpec((tm, tn), lambda g, j, i: (i, j)),
        (1, nj, m // tm),
        scratch=[pltpu.VMEM((CARRY_ROWS, tn), F32)], sub_rows=2 * SUB_ROWS, side_cast=side_cast,
        name="ffn_gate_up")


def shortconv_in(h, w_in, layer, conv_w, seq, tm=512, tn=512):
    m, k = h.shape
    d = w_in.shape[2] // 3
    nj = d // tn
    return matmul(
        [(h, pl.BlockSpec((tm, k), lambda g, j, i: (i, 0)))],
        [(w_in, _layer_cols(w_in, layer, tn, s * nj), 0) for s in range(3)],
        [(conv_w, pl.BlockSpec((conv_w.shape[0], tn), lambda g, j, i: (0, j)))],
        functools.partial(_ep_shortconv, bps=seq // tm),
        jax.ShapeDtypeStruct((m, d), BF16),
        pl.BlockSpec((tm, tn), lambda g, j, i: (i, j)),
        (1, nj, m // tm),
        scratch=[pltpu.VMEM((CARRY_ROWS, tn), F32)], name="shortconv_in")


def _whole_weight(w, layer):
    return pl.BlockSpec((None,) + w.shape[1:], lambda i: (layer, 0, 0), pipeline_mode=pl.Buffered(1))


def _norm_rows(o_ref, ssq, g_ref, h_ref):
    n = o_ref.shape[1]
    rstd = lax.rsqrt(ssq * (1.0 / n) + NORM_EPS)
    for c in range(n // SUB_COLS):
        cols = pl.ds(c * SUB_COLS, SUB_COLS)
        h_ref[:, cols] = (o_ref[:, cols] * rstd * g_ref[:, cols]).astype(h_ref.dtype)


def _proj_norm_body(y_ref, w_ref, res_ref, g_ref, o_ref, h_ref, wb_ref):
    @pl.when(pl.program_id(0) == 0)
    def _():
        _cast_weight(w_ref, wb_ref)

    if len(y_ref.shape) == 3:
        y = jnp.concatenate([y_ref[s] for s in range(y_ref.shape[0])], axis=1)
    else:
        y = y_ref[...]
    ssq = jnp.zeros((y.shape[0], 1), F32)
    for c in range(o_ref.shape[1] // SUB_COLS):
        cols = pl.ds(c * SUB_COLS, SUB_COLS)
        xn = res_ref[:, cols] + jnp.dot(y, wb_ref[:, cols], preferred_element_type=F32)
        o_ref[:, cols] = xn
        ssq = ssq + jnp.sum(xn * xn, axis=-1, keepdims=True)
    _norm_rows(o_ref, ssq, g_ref, h_ref)


def project_residual_norm(y, w, layer, residual, g, tm=512):
    k, n = w.shape[1:]
    m = residual.shape[0]
    row = lambda width: pl.BlockSpec((tm, width), lambda i: (i, 0))
    y_spec = row(k) if y.ndim == 2 else pl.BlockSpec((y.shape[0], tm, y.shape[2]), lambda i: (0, i, 0))
    return pl.pallas_call(
        _proj_norm_body,
        out_shape=(jax.ShapeDtypeStruct((m, n), F32), jax.ShapeDtypeStruct((m, n), BF16)),
        grid=(m // tm,),
        in_specs=[y_spec, _whole_weight(w, layer), row(n), pl.BlockSpec((1, n), lambda i: (0, 0))],
        out_specs=(row(n), row(n)),
        scratch_shapes=[pltpu.VMEM((k, n), BF16)],
        compiler_params=_params(("arbitrary",)),
        name="project_residual_norm",
    )(y, w, residual, g.reshape(1, n))


def _ple_body(x_ref, p_ref, wg_ref, wp_ref, g_ref, *refs, norm_next, keep_x):
    if norm_next and keep_x:
        gn_ref, o_ref, h_ref, wgb_ref, wpb_ref = refs
    elif norm_next:
        gn_ref, h_ref, wgb_ref, wpb_ref, o_ref = refs
    else:
        o_ref, wgb_ref, wpb_ref = refs

    @pl.when(pl.program_id(0) == 0)
    def _():
        _cast_weight(wg_ref, wgb_ref)
        _cast_weight(wp_ref, wpb_ref)

    hn = _rms(x_ref[...], g_ref[...]).astype(BF16)
    pb = p_ref[...].astype(BF16)
    ssq = jnp.zeros((hn.shape[0], 1), F32)
    for c in range(o_ref.shape[1] // SUB_COLS):
        cols = pl.ds(c * SUB_COLS, SUB_COLS)
        gate = jnp.dot(hn, wgb_ref[:, cols], preferred_element_type=F32)
        proj = jnp.dot(pb, wpb_ref[:, cols], preferred_element_type=F32)
        xn = x_ref[:, cols] + _sigmoid(gate) * proj
        o_ref[:, cols] = xn
        ssq = ssq + jnp.sum(xn * xn, axis=-1, keepdims=True)
    if norm_next:
        _norm_rows(o_ref, ssq, gn_ref, h_ref)


def ple_update(x, p, layer, w_gate, w_proj, g, g_next=None, next_dtype=BF16, keep_x=True, tm=512):
    m, d = x.shape
    kp = p.shape[2]
    norm_next = g_next is not None
    assert norm_next or keep_x
    row = pl.BlockSpec((tm, d), lambda i: (i, 0))
    vec = pl.BlockSpec((1, d), lambda i: (0, 0))
    operands = [x, p, w_gate, w_proj, g.reshape(1, d)] + ([g_next.reshape(1, d)] if norm_next else [])
    in_specs = [row, pl.BlockSpec((None, tm, kp), lambda i: (layer, i, 0)),
                _whole_weight(w_gate, layer), _whole_weight(w_proj, layer), vec, vec]
    out_shape = ([jax.ShapeDtypeStruct((m, d), F32)] if keep_x else []) + (
        [jax.ShapeDtypeStruct((m, d), next_dtype)] if norm_next else [])
    scratch = [pltpu.VMEM((d, d), BF16), pltpu.VMEM((kp, d), BF16)] + ([] if keep_x else [pltpu.VMEM((tm, d), F32)])
    return pl.pallas_call(
        functools.partial(_ple_body, norm_next=norm_next, keep_x=keep_x),
        out_shape=tuple(out_shape),
        grid=(m // tm,),
        in_specs=in_specs[:len(operands)],
        out_specs=(row,) * len(out_shape),
        scratch_shapes=scratch,
        compiler_params=_params(("arbitrary",)),
        name="ple_update",
    )(*operands)


def _attn_body(slopes_ref, *refs):
    qkv_refs = refs[:9]
    o_ref = refs[9]
    out_ref, lse_ref = refs[10:]
    head = pl.program_id(1)
    blk = ATTN_BLOCK

    qi = lax.broadcasted_iota(jnp.int32, (blk, 2 * blk), 0)
    kj = lax.broadcasted_iota(jnp.int32, (blk, 2 * blk), 1)
    dist = qi + blk - kj
    distf = dist.astype(F32)

    blocks = []
    for g, (window, dil) in enumerate(ATTN_GROUPS):
        seq = qkv_refs[3 * g].shape[0]
        nb = seq // dil // blk
        n_back = window // dil
        coef = slopes_ref[g, head] * (float(dil) * LOG2_E)
        bias2 = jnp.where((dist >= 0) & (dist <= n_back), -(coef * distf), NEG_INF)
        bias1 = bias2[:, blk:]
        for j in range(seq // blk):
            blocks.append((g, dil, j // nb, j % nb, j, bias1, bias2))

    def combine(first_row, n_rows):
        rows_per_step = 256
        for c in range(n_rows // rows_per_step):
            sl = pl.ds(first_row + c * rows_per_step, rows_per_step)
            lses = [lse_ref[g, sl, :] for g in range(3)]
            top = jnp.maximum(jnp.maximum(lses[0], lses[1]), lses[2])
            es = [jnp.exp2(x - top) for x in lses]
            num = es[0] * out_ref[0, sl, :] + es[1] * out_ref[1, sl, :] + es[2] * out_ref[2, sl, :]
            o_ref[sl, :] = (num / (es[0] + es[1] + es[2])).astype(o_ref.dtype)

    stage_rows = blk * ATTN_GROUPS[1][1]
    n_stages = o_ref.shape[0] // stage_rows
    wide = [b for b in blocks if blk * b[1] > stage_rows]
    batches = [(wide[b0:b0 + ATTN_BLOCKS_IN_FLIGHT], None) for b0 in range(0, len(wide), ATTN_BLOCKS_IN_FLIGHT)]
    for st in range(n_stages):
        batches.append(([b for b in blocks if blk * b[1] <= stage_rows and (b[3] * blk * b[1]) // stage_rows == st],
                        st))

    for batch, stage in batches:
        s_list, v_list = [], []
        for g, dil, r, n, j, bias1, bias2 in batch:
            q_ref, k_ref, v_ref = qkv_refs[3 * g:3 * g + 3]
            lo = j * blk if n == 0 else (j - 1) * blk
            s = lax.dot_general(q_ref[j * blk:(j + 1) * blk, :], k_ref[lo:(j + 1) * blk, :],
                                (((1,), (1,)), ((), ())), preferred_element_type=F32)
            s_list.append(s + (bias1 if n == 0 else bias2))
            v_list.append(v_ref[lo:(j + 1) * blk, :])
        m_list = [jnp.max(s, axis=1, keepdims=True) for s in s_list]
        p_list = [jnp.exp2(s - m) for s, m in zip(s_list, m_list)]
        l_list = [jnp.sum(p, axis=1, keepdims=True) for p in p_list]
        acc_list = [jnp.dot(p.astype(BF16), v, preferred_element_type=F32) for p, v in zip(p_list, v_list)]
        for (g, dil, r, n, j, _, _), m, l, acc in zip(batch, m_list, l_list, acc_list):
            start = n * blk * dil + r
            rows = pl.ds(start, blk) if dil == 1 else pl.ds(start, blk, stride=dil)
            out_ref[g, rows, :] = acc * (1.0 / l)
            lse_ref[g, rows, :] = jnp.broadcast_to(m + jnp.log2(l), acc.shape)
        if stage is not None:
            combine(stage * stage_rows, stage_rows)


def dilated_attention(qkv, slopes, batch, seq):
    n_g = len(qkv)
    m = qkv[0].shape[1]
    hd = ATTN_HEAD_DIM
    in_specs = [pl.BlockSpec(memory_space=pltpu.SMEM)]
    for g in range(n_g):
        for c in range(3):
            in_specs.append(pl.BlockSpec((None, seq, hd), lambda b, h, c=c: (c * ATTN_HEADS + h, b, 0)))
    return pl.pallas_call(
        _attn_body,
        out_shape=jax.ShapeDtypeStruct((ATTN_HEADS, m, hd), BF16),
        grid=(batch, ATTN_HEADS),
        in_specs=in_specs,
        out_specs=pl.BlockSpec((None, seq, hd), lambda b, h: (h, b, 0)),
        scratch_shapes=[pltpu.VMEM((n_g, seq, hd), F32)] * 2,
        compiler_params=_params(("arbitrary", "arbitrary")),
        name="dilated_attention",
    )(slopes, *[a for a in qkv for _ in range(3)])


def _split3(x):
    hi = x.astype(BF16)
    r1 = x - hi.astype(F32)
    mid = r1.astype(BF16)
    lo = (r1 - mid.astype(F32)).astype(BF16)
    return hi, mid, lo


def _dot_split2_rhs(x, rhs_bf16):
    hi = x.astype(BF16)
    lo = (x - hi.astype(F32)).astype(BF16)
    return jnp.dot(hi, rhs_bf16, preferred_element_type=F32) + jnp.dot(lo, rhs_bf16, preferred_element_type=F32)


def _dot_exact_lhs(lhs_bf16, x):
    return sum(jnp.dot(lhs_bf16, part, preferred_element_type=F32) for part in _split3(x))


def _dot_t(a, b):
    return lax.dot_general(a, b, (((1,), (1,)), ((), ())), preferred_element_type=F32)


def _t_dot(a, b):
    return lax.dot_general(a, b, (((0,), (0,)), ((), ())), preferred_element_type=F32)


def _scan_chunk_body(r_ref, k_ref, v_ref, a_ref, lw_ref, g_ref, kk_ref, ka_ref, rk_ref, lng_ref, lnb_ref,
                     o_ref, state_ref, lhs_ref, rhs_ref, vs_ref, tail_ref):
    t = SCAN_CHUNK
    n_sub = r_ref.shape[0] // t
    n_pairs = r_ref.shape[1] // LANES
    pairs = range(n_pairs)

    @pl.when(pl.program_id(1) == 0)
    def _():
        state_ref[...] = jnp.zeros_like(state_ref)

    lane = lax.broadcasted_iota(jnp.int32, (t, LANES), 1)
    head0 = lane < RWKV_HEAD_SIZE
    ri = lax.broadcasted_iota(jnp.int32, (2 * t, 2 * t), 0)
    ci = lax.broadcasted_iota(jnp.int32, (2 * t, 2 * t), 1)
    same_head = (ri < t) == (ci < t)
    strict = same_head & (ci < ri)
    incl = same_head & (ci <= ri)
    eye = jnp.where(ri == ci, 1.0, 0.0)
    tri = jnp.where(lax.broadcasted_iota(jnp.int32, (t, t), 1) <= lax.broadcasted_iota(jnp.int32, (t, t), 0),
                    1.0, 0.0).astype(BF16)
    li = lax.broadcasted_iota(jnp.int32, (LANES, LANES), 0)
    lj = lax.broadcasted_iota(jnp.int32, (LANES, LANES), 1)
    head_ones = jnp.where((li < RWKV_HEAD_SIZE) == (lj < RWKV_HEAD_SIZE), 1.0, 0.0).astype(BF16)
    zeros = jnp.zeros_like(head_ones)
    head_ones2 = jnp.concatenate([jnp.concatenate([head_ones, zeros], axis=1),
                                  jnp.concatenate([zeros, head_ones], axis=1)], axis=0)
    n_doublings = (t - 1).bit_length()

    def lanes(x, p):
        return x[:, p * LANES:(p + 1) * LANES]

    def stack(x):
        return jnp.concatenate([jnp.where(head0, x, 0.0), jnp.where(head0, 0.0, x)], axis=0)

    def tall(x):
        return jnp.concatenate([lanes(x, p) for p in pairs], axis=0)

    def wide(s):
        return jnp.concatenate([s[p * t:(p + 1) * t, :] for p in pairs], axis=1)

    def head_sum(x):
        return wide(_dot_split2_rhs(tall(x), head_ones))

    def head_sum_two(x1, x2):
        s = _dot_split2_rhs(jnp.concatenate([tall(x1), tall(x2)], axis=1), head_ones2)
        return wide(s[:, :LANES]), wide(s[:, LANES:])

    def mm(a, b):
        return jnp.dot(a, b, preferred_element_type=F32)

    inv_n = 1.0 / RWKV_HEAD_SIZE

    def operands(c):
        rows = pl.ds(c * t, t)
        r = r_ref[rows, :]
        k0 = k_ref[rows, :]
        a_gate = a_ref[rows, :]
        lw = lw_ref[rows, :]
        kk = k0 * kk_ref[...]
        k = k0 * (1.0 + (a_gate - 1.0) * ka_ref[...])
        kk_sq, rk_sum = head_sum_two(kk * kk, r * k * rk_ref[...])
        kk = kk / jnp.maximum(jnp.sqrt(kk_sq), 1e-12)
        b = kk * a_gate
        lg = _dot_exact_lhs(tri, lw)
        g_incl = jnp.exp(lg)
        g_inv = jnp.exp(-lg)
        g_last = g_incl[t - 1:t, :]
        g_tail = g_last * g_inv
        at = -kk * jnp.exp(lg - lw)
        rt = r * g_incl
        bt = b * g_inv
        kt = k * g_inv
        bh = b * g_tail
        kh = k * g_tail
        for p in pairs:
            lhs_ref[c, p, :2 * t, :] = stack(lanes(at, p)).astype(BF16)
            lhs_ref[c, p, 2 * t:, :] = stack(lanes(rt, p)).astype(BF16)
            btp = lanes(bt, p).astype(BF16)
            ktp = lanes(kt, p).astype(BF16)
            rhs_ref[c, p] = jnp.concatenate([btp, btp, ktp, ktp], axis=0)
            vs_ref[c, p] = stack(lanes(v_ref[rows, :], p)).astype(BF16)
            tail_ref[c, p, :2 * t, :] = stack(lanes(bh, p)).astype(BF16)
            tail_ref[c, p, 2 * t:, :] = stack(lanes(kh, p)).astype(BF16)
        return g_last, rk_sum

    def triangular(c):
        n_ab, n_ak, b_r = [], [], []
        for p in pairs:
            sc = _dot_t(lhs_ref[c, p], rhs_ref[c, p])
            n_ab.append(jnp.where(strict, sc[:2 * t, :2 * t], 0.0))
            n_ak.append(jnp.where(strict, sc[:2 * t, 2 * t:], 0.0).astype(BF16))
            b_r.append(jnp.concatenate([jnp.where(incl, sc[2 * t:, :2 * t], 0.0),
                                        jnp.where(incl, sc[2 * t:, 2 * t:], 0.0)], axis=1).astype(BF16))
        inv = [eye + n for n in n_ab]
        pw = n_ab
        for _ in range(n_doublings - 1):
            pwb = [x.astype(BF16) for x in pw]
            pw = [mm(x, x) for x in pwb]
            inv = [i_p + mm(q.astype(BF16), i_p.astype(BF16)) for i_p, q in zip(inv, pw)]
        return n_ak, b_r, [x.astype(BF16) for x in inv]

    def advance(c, g_last, n_ak, b_r, inv):
        state = [state_ref[p] for p in pairs]
        state_b = [s.astype(BF16) for s in state]
        w = [_dot_t(lhs_ref[c, p, :2 * t, :], state_b[p]) + mm(n_ak[p], vs_ref[c, p]) for p in pairs]
        zv = [jnp.concatenate([mm(inv[p], w[p].astype(BF16)).astype(BF16), vs_ref[c, p]], axis=0) for p in pairs]
        y_parts = []
        for p in pairs:
            y_s = _dot_t(lhs_ref[c, p, 2 * t:, :], state_b[p]) + mm(b_r[p], zv[p])
            y_parts.append(y_s[:t, :] + y_s[t:, :])
            state_ref[p] = state[p] * lanes(g_last, p) + _t_dot(zv[p], tail_ref[c, p])
        return jnp.concatenate(y_parts, axis=1)

    def finish(c, y, y_sum, rk_sum):
        rows = pl.ds(c * t, t)
        yc = y - y_sum * inv_n
        var = head_sum(yc * yc) * inv_n
        yn = yc * lax.rsqrt(var + RWKV_GN_EPS) * lng_ref[...] + lnb_ref[...]
        o_ref[rows, :] = ((yn + rk_sum * v_ref[rows, :]) * g_ref[rows, :]).astype(o_ref.dtype)

    prepared = [operands(c) for c in range(n_sub)]
    solved = [triangular(c) for c in range(n_sub)]
    pending = None
    for c in range(n_sub):
        y = advance(c, prepared[c][0], *solved[c])
        if pending is not None:
            finish(*pending)
        pending = (c, y, head_sum(y), prepared[c][1])
    finish(*pending)


def rwkv_scan(rkv, a_gate, log_w, gate, k_k, k_a, r_k, ln_g, ln_b, batch, seq):
    _, m, d = rkv.shape
    t = SCAN_CHUNK
    n_sub = SCAN_CHUNKS_PER_STEP
    nc = seq // (n_sub * t)
    n_pairs = d // LANES
    row = lambda g: pl.BlockSpec((None, n_sub * t, d), lambda b, c, g=g: (g, b * nc + c, 0))
    tile = pl.BlockSpec((n_sub * t, d), lambda b, c: (b * nc + c, 0))
    vec = pl.BlockSpec((1, d), lambda b, c: (0, 0))
    return pl.pallas_call(
        _scan_chunk_body,
        out_shape=jax.ShapeDtypeStruct((m, d), BF16),
        grid=(batch, nc),
        in_specs=[row(0), row(1), row(2), tile, tile, tile, vec, vec, vec, vec, vec],
        out_specs=tile,
        scratch_shapes=[pltpu.VMEM((n_pairs, LANES, LANES), F32),
                        pltpu.VMEM((n_sub, n_pairs, 4 * t, LANES), BF16),
                        pltpu.VMEM((n_sub, n_pairs, 4 * t, LANES), BF16),
                        pltpu.VMEM((n_sub, n_pairs, 2 * t, LANES), BF16),
                        pltpu.VMEM((n_sub, n_pairs, 4 * t, LANES), BF16)],
        compiler_params=_params(("arbitrary", "arbitrary")),
        name="rwkv_scan",
    )(rkv, rkv, rkv, a_gate, log_w, gate,
      k_k.reshape(1, d), k_a.reshape(1, d), r_k.reshape(1, d), ln_g.reshape(1, d), ln_b.reshape(1, d))


def _alibi_slopes(n_groups, n_heads):
    n = n_groups * n_heads
    idx = jnp.arange(1, n + 1, dtype=F32)
    return (2.0 ** (-8.0 * idx / n)).reshape(n_groups, n_heads)


def kernel(x, p, attn_norm, attn_w_qkv, attn_w_o, rwkv_norm, rwkv_mu, rwkv_w_rkv, rwkv_w0, rwkv_w_w1, rwkv_w_w2, rwkv_a0, rwkv_w_a1, rwkv_w_a2, rwkv_w_g1, rwkv_w_g2, rwkv_k_k, rwkv_k_a, rwkv_r_k, rwkv_ln_g, rwkv_ln_b, rwkv_w_o, conv_norm, conv_w_in, conv_w, conv_w_out, ffn_norm, ffn_w_gu, ffn_conv_w, ffn_conv_b, ffn_w_down, ple_w_proj, ple_norm, ple_w_gate, final_norm):
    batch, seq, d = x.shape
    depth = p.shape[0]
    m = batch * seq
    n_groups = len(ATTN_GROUPS)
    slopes = _alibi_slopes(n_groups, ATTN_HEADS)
    x = x.reshape(m, d)
    p = p.reshape(depth, m, p.shape[-1])

    h_next = None
    for i in range(depth):
        kind, j = i % 3, i // 3
        if kind == 0:
            hs = attn_rmsnorm(x, attn_norm[j], batch, seq)
            n_q = ATTN_HEADS * ATTN_HEAD_DIM
            q_scale = jnp.concatenate([jnp.full((1, n_q), ATTN_HEAD_DIM ** -0.5 * LOG2_E, F32),
                                       jnp.ones((1, 2 * n_q), F32)], axis=1)
            qkv = [linear(hs[g], attn_w_qkv, j, BF16, tm=2048, n_out=3 * n_q, first_col=g * 3 * n_q,
                          col_scale=q_scale, slab_major=True, name="attn_qkv") for g in range(n_groups)]
            y = dilated_attention(qkv, slopes, batch, seq)
            w_out = attn_w_o
        elif kind == 1:
            xs, log_w, a_gate, gate = rwkv_prep(
                x, rwkv_norm[j], rwkv_mu[j], rwkv_w0[j], rwkv_a0[j],
                [rwkv_w_w1[j], rwkv_w_w2[j], rwkv_w_a1[j], rwkv_w_a2[j], rwkv_w_g1[j], rwkv_w_g2[j]], seq)
            rkv = grouped_linear(xs, rwkv_w_rkv, j, F32, tm=2048, tn=1024, name="rwkv_rkv")
            y = rwkv_scan(rkv, a_gate, log_w, gate, rwkv_k_k[j], rwkv_k_a[j], rwkv_r_k[j].reshape(d),
                          rwkv_ln_g[j], rwkv_ln_b[j], batch, seq)
            w_out = rwkv_w_o
        else:
            y = shortconv_in(h_next, conv_w_in, j, conv_w[j], seq, tm=2048, tn=256)
            w_out = conv_w_out
        x, h = project_residual_norm(y, w_out, j, x, ffn_norm[i])
        act, w_down = ffn_gate_up(h, ffn_w_gu, i, ffn_conv_w[i], ffn_conv_b[i], seq, ffn_w_down, tm=2048)
        x = linear(act, w_down, None, F32, tm=512, tn=1024, residual=x, name="ffn_down")
        if i + 1 == depth:
            out, = ple_update(x, p, i, ple_w_gate, ple_w_proj, ple_norm[i], final_norm, F32, keep_x=False)
            return out.reshape(batch, seq, d)
        if (i + 1) % 3 == 2:
            x, h_next = ple_update(x, p, i, ple_w_gate, ple_w_proj, ple_norm[i], conv_norm[(i + 1) // 3])
        else:
            x, = ple_update(x, p, i, ple_w_gate, ple_w_proj, ple_norm[i])
```

```python
import functools

import jax
import jax.numpy as jnp
from jax import lax
from jax.experimental import pallas as pl
from jax.experimental.pallas import tpu as pltpu

F32 = jnp.float32
BF16 = jnp.bfloat16

NORM_EPS = 1e-6
NEG_INF = -1e30
RWKV_GN_EPS = 6.4e-4

ATTN_GROUPS = ((128, 1), (512, 4), (2048, 16))
ATTN_HEADS = 16
ATTN_HEAD_DIM = 128
ATTN_BLOCK = 128
ATTN_BLOCKS_IN_FLIGHT = 16
ATTN_HEADS_PER_STEP = 2
LOG2_E = 1.4426950408889634
EXP_MINUS_HALF = 0.6065306597126334
RWKV_HEAD_SIZE = 64
SCAN_CHUNK = 64
SCAN_CHUNKS_PER_STEP = 2
LANES = 128
CARRY_ROWS = 8
SUB_ROWS = 512
SUB_COLS = 256

VMEM_LIMIT_BYTES = 56 * 1024 * 1024


def _params(semantics):
    return pltpu.CompilerParams(dimension_semantics=semantics, vmem_limit_bytes=VMEM_LIMIT_BYTES)


def _sigmoid(x):
    return 1.0 / (1.0 + jnp.exp(-x))


def _rms(x, g):
    ms = jnp.mean(x * x, axis=-1, keepdims=True)
    return x * lax.rsqrt(ms + NORM_EPS) * g


def _shift_rows(cur, prev_tail, n):
    row = lax.broadcasted_iota(jnp.int32, cur.shape, 0)
    out = pltpu.roll(cur, n, axis=0)
    for r in range(n):
        src = CARRY_ROWS - n + r
        out = jnp.where(row == r, prev_tail[src:src + 1, :], out)
    return out


def _attn_norm_body(x_ref, g_ref, o0_ref, o1_ref, o2_ref, y_ref, *, tm):
    y = _rms(x_ref[...], g_ref[...])
    o0_ref[...] = y.astype(o0_ref.dtype)
    for c in range(y_ref.shape[0]):
        y_ref[c] = y[:, c * LANES:(c + 1) * LANES]
    for o_ref, dil in ((o1_ref, ATTN_GROUPS[1][1]), (o2_ref, ATTN_GROUPS[2][1])):
        for r in range(dil):
            for c in range(y_ref.shape[0]):
                o_ref[r, :, c * LANES:(c + 1) * LANES] = (
                    y_ref[c, pl.ds(r, tm // dil, stride=dil), :].astype(o_ref.dtype))


def attn_rmsnorm(x, g, batch, seq, tm=512):
    m, d = x.shape
    bps = seq // tm
    d1, d2 = ATTN_GROUPS[1][1], ATTN_GROUPS[2][1]
    outs = pl.pallas_call(
        functools.partial(_attn_norm_body, tm=tm),
        out_shape=(jax.ShapeDtypeStruct((m, d), BF16),
                   jax.ShapeDtypeStruct((batch, d1, seq // d1, d), BF16),
                   jax.ShapeDtypeStruct((batch, d2, seq // d2, d), BF16)),
        grid=(batch, bps),
        in_specs=[pl.BlockSpec((tm, d), lambda b, i: (b * bps + i, 0)),
                  pl.BlockSpec((1, d), lambda b, i: (0, 0))],
        out_specs=(pl.BlockSpec((tm, d), lambda b, i: (b * bps + i, 0)),
                   pl.BlockSpec((None, d1, tm // d1, d), lambda b, i: (b, 0, i, 0)),
                   pl.BlockSpec((None, d2, tm // d2, d), lambda b, i: (b, 0, i, 0))),
        scratch_shapes=[pltpu.VMEM((d // LANES, tm, LANES), F32)],
        compiler_params=_params(("arbitrary", "arbitrary")),
        name="attn_norm",
    )(x, g.reshape(1, d))
    return [o.reshape(m, d) for o in outs]


def _rwkv_prep_body(x_ref, g_ref, mu_ref, w0_ref, a0_ref, *refs, bps):
    lora_refs = refs[:6]
    rkv_ref, lw_ref, a_ref, gate_ref = refs[6:10]
    carry_ref = refs[10]
    lora_b = refs[11:17]

    @pl.when(pl.program_id(0) == 0)
    def _():
        for w_ref, wb_ref in zip(lora_refs, lora_b):
            _cast_weight(w_ref, wb_ref)

    @pl.when(pl.program_id(0) % bps == 0)
    def _():
        carry_ref[...] = jnp.zeros_like(carry_ref)

    h = _rms(x_ref[...], g_ref[...])
    xx = _shift_rows(h, carry_ref[...], 1) - h
    carry_ref[...] = h[h.shape[0] - CARRY_ROWS:, :]

    def mix(n):
        return (h + xx * mu_ref[n:n + 1, :]).astype(BF16)

    def low_rank(n, w1b_ref, w2b_ref, mid_fn):
        mid = mid_fn(jnp.dot(mix(n), w1b_ref[...], preferred_element_type=F32))
        return jnp.dot(mid.astype(BF16), w2b_ref[...], preferred_element_type=F32)

    for n in range(3):
        rkv_ref[n] = mix(n)
    u = w0_ref[...] + low_rank(3, lora_b[0], lora_b[1], jnp.tanh)
    lw_ref[...] = -EXP_MINUS_HALF * _sigmoid(u)
    a_ref[...] = _sigmoid(a0_ref[...] + low_rank(4, lora_b[2], lora_b[3], lambda t: t))
    gate_ref[...] = low_rank(5, lora_b[4], lora_b[5], _sigmoid)


def rwkv_prep(x, g, mu, w0, a0, loras, seq, tm=256):
    m, d = x.shape
    padded = []
    for idx in range(0, len(loras), 2):
        w1, w2 = loras[idx], loras[idx + 1]
        r = -(-w1.shape[1] // LANES) * LANES
        padded += [jnp.pad(w1, ((0, 0), (0, r - w1.shape[1]))), jnp.pad(w2, ((0, r - w2.shape[0]), (0, 0)))]
    row = pl.BlockSpec((tm, d), lambda i: (i, 0))
    vec = pl.BlockSpec((1, d), lambda i: (0, 0))
    whole = lambda a: pl.BlockSpec(a.shape, lambda i: (0,) * a.ndim, pipeline_mode=pl.Buffered(1))
    return pl.pallas_call(
        functools.partial(_rwkv_prep_body, bps=seq // tm),
        out_shape=(jax.ShapeDtypeStruct((3, m, d), BF16),) + (jax.ShapeDtypeStruct((m, d), F32),) * 3,
        grid=(m // tm,),
        in_specs=[row, vec, whole(mu), vec, vec] + [whole(w) for w in padded],
        out_specs=(pl.BlockSpec((3, tm, d), lambda i: (0, i, 0)), row, row, row),
        scratch_shapes=[pltpu.VMEM((CARRY_ROWS, d), F32)] + [pltpu.VMEM(w.shape, BF16) for w in padded],
        compiler_params=_params(("arbitrary",)),
        name="rwkv_prep",
    )(x, g.reshape(1, d), mu, w0.reshape(1, d), a0.reshape(1, d), *padded)


def _cast_weight(w_ref, wb_ref, rows=256):
    k = w_ref.shape[0]
    rows = min(rows, k)

    def chunk(c, carry):
        sl = pl.ds(pl.multiple_of(c * rows, rows), rows)
        wb_ref[sl, :] = w_ref[sl, :].astype(wb_ref.dtype)
        return carry

    lax.fori_loop(0, k // rows, chunk, 0)


def _mm_body(*refs, n_x, w_lhs, w_is_f32, n_extra, n_scratch, epilogue, sub_rows, side_cast):
    refs = list(refs)
    x_refs = [refs.pop(0) for _ in range(n_x)]
    w_refs = [refs.pop(0) for _ in w_lhs]
    e_refs = [refs.pop(0) for _ in range(n_extra)]
    side_in = refs.pop(0) if side_cast else None
    o_ref = refs.pop(0)
    side_out = refs.pop(0) if side_cast else None
    cast_refs = [refs.pop(0) for f in w_is_f32 if f]
    s_refs = refs
    assert len(s_refs) == n_scratch
    cast_iter = iter(cast_refs)
    wb_refs = [next(cast_iter) if f else w_ref for w_ref, f in zip(w_refs, w_is_f32)]

    @pl.when(pl.program_id(2) == 0)
    def _():
        for w_ref, wb_ref, f in zip(w_refs, wb_refs, w_is_f32):
            if f:
                _cast_weight(w_ref, wb_ref)

    if side_cast:
        side_out[...] = side_in[...].astype(side_out.dtype)

    tm = o_ref.shape[-2]
    sub = min(sub_rows, tm)
    for c in range(tm // sub):
        rows = pl.ds(c * sub, sub)
        xs = [x_ref[rows, :].astype(BF16) for x_ref in x_refs]
        accs = [jnp.dot(xs[l], wb_ref[...], preferred_element_type=F32)
                for wb_ref, l in zip(wb_refs, w_lhs)]
        out = epilogue(accs, e_refs, s_refs, rows, c == 0).astype(o_ref.dtype)
        if len(o_ref.shape) == 2:
            o_ref[rows, :] = out
        else:
            for slab in range(o_ref.shape[0]):
                o_ref[slab, rows, :] = out[:, slab * LANES:(slab + 1) * LANES]


def matmul(xs, ws, extras, epilogue, out_shape, out_spec, grid, scratch=(), sub_rows=SUB_ROWS, side_cast=None,
           name="matmul"):
    w_lhs = tuple(l for _, _, l in ws)
    w_is_f32 = tuple(a.dtype == F32 for a, _, _ in ws)
    wb_scratch = [pltpu.VMEM(spec.block_shape[-2:], BF16) for (_, spec, _), f in zip(ws, w_is_f32) if f]
    body = functools.partial(_mm_body, n_x=len(xs), w_lhs=w_lhs, w_is_f32=w_is_f32, n_extra=len(extras),
                             n_scratch=len(scratch), epilogue=epilogue, sub_rows=sub_rows,
                             side_cast=side_cast is not None)
    in_specs = [s for _, s in xs] + [s for _, s, _ in ws] + [s for _, s in extras]
    operands = [a for a, _ in xs] + [a for a, _, _ in ws] + [a for a, _ in extras]
    if side_cast is not None:
        in_specs.append(side_cast[1])
        operands.append(side_cast[0])
        out_shape, out_spec = (out_shape, side_cast[2]), (out_spec, side_cast[3])
    return pl.pallas_call(
        body,
        out_shape=out_shape,
        grid=grid,
        in_specs=in_specs,
        out_specs=out_spec,
        scratch_shapes=wb_scratch + list(scratch),
        compiler_params=_params(("arbitrary", "arbitrary", "arbitrary")),
        name=name,
    )(*operands)


def _ep_plain(accs, e_refs, s_refs, rows, first_sub):
    return accs[0]


def _ep_residual(accs, e_refs, s_refs, rows, first_sub):
    return e_refs[0][rows, :] + accs[0]


def _ep_col_scale(accs, e_refs, s_refs, rows, first_sub):
    return accs[0] * e_refs[0][...]


def _conv3(cur, cw_ref, carry_ref, bps, first_sub):
    if first_sub:
        @pl.when(pl.program_id(2) % bps == 0)
        def _():
            carry_ref[...] = jnp.zeros_like(carry_ref)

    tail = carry_ref[...]
    d1 = _shift_rows(cur, tail, 1)
    d2 = _shift_rows(cur, tail, 2)
    carry_ref[...] = cur[cur.shape[0] - CARRY_ROWS:, :]
    w = cw_ref[...]
    return d2 * w[0:1, :] + d1 * w[1:2, :] + cur * w[2:3, :]


def _ep_ffn(accs, e_refs, s_refs, rows, first_sub, *, bps):
    gate = _conv3(accs[0], e_refs[0], s_refs[0], bps, first_sub) + e_refs[1][...]
    return gate * _sigmoid(gate) * accs[1]


def _ep_shortconv(accs, e_refs, s_refs, rows, first_sub, *, bps):
    return accs[0] * _conv3(accs[1] * accs[2], e_refs[0], s_refs[0], bps, first_sub)


def _layer_cols(w, layer, tn, first_block=0):
    if layer is None:
        return pl.BlockSpec((w.shape[0], tn), lambda g, j, i: (0, first_block + j))
    return pl.BlockSpec((None, w.shape[1], tn), lambda g, j, i: (layer, 0, first_block + j))


def linear(x, w, layer, out_dtype, tm=512, tn=1024, n_out=None, first_col=0, residual=None, col_scale=None,
           slab_major=False, name="linear"):
    m, k = x.shape
    n = n_out or w.shape[-1]
    tn = min(tn, n)
    extras = []
    epilogue = _ep_plain
    assert residual is None or col_scale is None
    if residual is not None:
        extras = [(residual, pl.BlockSpec((tm, tn), lambda g, j, i: (i, j)))]
        epilogue = _ep_residual
    if col_scale is not None:
        extras = [(col_scale, pl.BlockSpec((1, tn), lambda g, j, i: (0, j)))]
        epilogue = _ep_col_scale
    if slab_major:
        out_shape = jax.ShapeDtypeStruct((n // LANES, m, LANES), out_dtype)
        out_spec = pl.BlockSpec((tn // LANES, tm, LANES), lambda g, j, i: (j, i, 0))
    else:
        out_shape = jax.ShapeDtypeStruct((m, n), out_dtype)
        out_spec = pl.BlockSpec((tm, tn), lambda g, j, i: (i, j))
    return matmul(
        [(x, pl.BlockSpec((tm, k), lambda g, j, i: (i, 0)))],
        [(w, _layer_cols(w, layer, tn, first_col // tn), 0)],
        extras, epilogue, out_shape, out_spec,
        (1, n // tn, m // tm), name=name)


def grouped_linear(x, w, layer, out_dtype, tm=512, tn=1024, name="grouped_linear"):
    m, k = x.shape[1:]
    n_groups, _, n_out = w.shape[1:]
    tn = min(tn, n_out)
    return matmul(
        [(x, pl.BlockSpec((None, tm, k), lambda g, j, i: (g, i, 0)))],
        [(w, pl.BlockSpec((None, None, k, tn), lambda g, j, i: (layer, g, 0, j)), 0)],
        [], _ep_plain,
        jax.ShapeDtypeStruct((n_groups, m, n_out), out_dtype),
        pl.BlockSpec((None, tm, tn), lambda g, j, i: (g, i, j)),
        (n_groups, n_out // tn, m // tm), name=name)


def ffn_gate_up(h, w_gu, layer, conv_w, conv_b, seq, w_down, tm=512, tn=512):
    m, k = h.shape
    f = w_gu.shape[2] // 2
    nj = f // tn
    ni = m // tm
    vec = lambda rows: pl.BlockSpec((rows, tn), lambda g, j, i: (0, j))
    kd, nd = w_down.shape[1:]
    rows_per_step = kd // (nj * ni)
    assert rows_per_step * nj * ni == kd and rows_per_step % 16 == 0
    side_cast = (w_down, pl.BlockSpec((None, rows_per_step, nd), lambda g, j, i: (layer, j * ni + i, 0)),
                 jax.ShapeDtypeStruct((kd, nd), BF16),
                 pl.BlockSpec((rows_per_step, nd), lambda g, j, i: (j * ni + i, 0)))
    return matmul(
        [(h, pl.BlockSpec((tm, k), lambda g, j, i: (i, 0)))],
        [(w_gu, _layer_cols(w_gu, layer, tn), 0),
         (w_gu, _layer_cols(w_gu, layer, tn, nj), 0)],
        [(conv_w, vec(conv_w.shape[0])), (conv_b.reshape(1, f), vec(1))],
        functools.partial(_ep_ffn, bps=seq // tm),
        jax.ShapeDtypeStruct((m, f), BF16),
        pl.BlockSpec((tm, tn), lambda g, j, i: (i, j)),
        (1, nj, m // tm),
        scratch=[pltpu.VMEM((CARRY_ROWS, tn), F32)], sub_rows=2 * SUB_ROWS, side_cast=side_cast,
        name="ffn_gate_up")


def shortconv_in(h, w_in, layer, conv_w, seq, tm=512, tn=512):
    m, k = h.shape
    d = w_in.shape[2] // 3
    nj = d // tn
    return matmul(
        [(h, pl.BlockSpec((tm, k), lambda g, j, i: (i, 0)))],
        [(w_in, _layer_cols(w_in, layer, tn, s * nj), 0) for s in range(3)],
        [(conv_w, pl.BlockSpec((conv_w.shape[0], tn), lambda g, j, i: (0, j)))],
        functools.partial(_ep_shortconv, bps=seq // tm),
        jax.ShapeDtypeStruct((m, d), BF16),
        pl.BlockSpec((tm, tn), lambda g, j, i: (i, j)),
        (1, nj, m // tm),
        scratch=[pltpu.VMEM((CARRY_ROWS, tn), F32)], name="shortconv_in")


def _whole_weight(w, layer):
    return pl.BlockSpec((None,) + w.shape[1:], lambda i: (layer, 0, 0), pipeline_mode=pl.Buffered(1))


def _norm_rows(o_ref, ssq, g_ref, h_ref):
    n = o_ref.shape[1]
    rstd = lax.rsqrt(ssq * (1.0 / n) + NORM_EPS)
    for c in range(n // SUB_COLS):
        cols = pl.ds(c * SUB_COLS, SUB_COLS)
        h_ref[:, cols] = (o_ref[:, cols] * rstd * g_ref[:, cols]).astype(h_ref.dtype)


def _proj_norm_body(y_ref, w_ref, res_ref, g_ref, o_ref, h_ref, wb_ref):
    @pl.when(pl.program_id(0) == 0)
    def _():
        _cast_weight(w_ref, wb_ref)

    if len(y_ref.shape) == 3:
        y = jnp.concatenate([y_ref[s] for s in range(y_ref.shape[0])], axis=1)
    else:
        y = y_ref[...]
    ssq = jnp.zeros((y.shape[0], 1), F32)
    for c in range(o_ref.shape[1] // SUB_COLS):
        cols = pl.ds(c * SUB_COLS, SUB_COLS)
        xn = res_ref[:, cols] + jnp.dot(y, wb_ref[:, cols], preferred_element_type=F32)
        o_ref[:, cols] = xn
        ssq = ssq + jnp.sum(xn * xn, axis=-1, keepdims=True)
    _norm_rows(o_ref, ssq, g_ref, h_ref)


def project_residual_norm(y, w, layer, residual, g, tm=512):
    k, n = w.shape[1:]
    m = residual.shape[0]
    row = lambda width: pl.BlockSpec((tm, width), lambda i: (i, 0))
    y_spec = row(k) if y.ndim == 2 else pl.BlockSpec((y.shape[0], tm, y.shape[2]), lambda i: (0, i, 0))
    return pl.pallas_call(
        _proj_norm_body,
        out_shape=(jax.ShapeDtypeStruct((m, n), F32), jax.ShapeDtypeStruct((m, n), BF16)),
        grid=(m // tm,),
        in_specs=[y_spec, _whole_weight(w, layer), row(n), pl.BlockSpec((1, n), lambda i: (0, 0))],
        out_specs=(row(n), row(n)),
        scratch_shapes=[pltpu.VMEM((k, n), BF16)],
        compiler_params=_params(("arbitrary",)),
        name="project_residual_norm",
    )(y, w, residual, g.reshape(1, n))


def _ple_body(x_ref, p_ref, wg_ref, wp_ref, g_ref, *refs, norm_next, keep_x):
    if norm_next and keep_x:
        gn_ref, o_ref, h_ref, wgb_ref, wpb_ref = refs
    elif norm_next:
        gn_ref, h_ref, wgb_ref, wpb_ref, o_ref = refs
    else:
        o_ref, wgb_ref, wpb_ref = refs

    @pl.when(pl.program_id(0) == 0)
    def _():
        _cast_weight(wg_ref, wgb_ref)
        _cast_weight(wp_ref, wpb_ref)

    hn = _rms(x_ref[...], g_ref[...]).astype(BF16)
    pb = p_ref[...].astype(BF16)
    ssq = jnp.zeros((hn.shape[0], 1), F32)
    for c in range(o_ref.shape[1] // SUB_COLS):
        cols = pl.ds(c * SUB_COLS, SUB_COLS)
        gate = jnp.dot(hn, wgb_ref[:, cols], preferred_element_type=F32)
        proj = jnp.dot(pb, wpb_ref[:, cols], preferred_element_type=F32)
        xn = x_ref[:, cols] + _sigmoid(gate) * proj
        o_ref[:, cols] = xn
        ssq = ssq + jnp.sum(xn * xn, axis=-1, keepdims=True)
    if norm_next:
        _norm_rows(o_ref, ssq, gn_ref, h_ref)


def ple_update(x, p, layer, w_gate, w_proj, g, g_next=None, next_dtype=BF16, keep_x=True, tm=512):
    m, d = x.shape
    kp = p.shape[2]
    norm_next = g_next is not None
    assert norm_next or keep_x
    row = pl.BlockSpec((tm, d), lambda i: (i, 0))
    vec = pl.BlockSpec((1, d), lambda i: (0, 0))
    operands = [x, p, w_gate, w_proj, g.reshape(1, d)] + ([g_next.reshape(1, d)] if norm_next else [])
    in_specs = [row, pl.BlockSpec((None, tm, kp), lambda i: (layer, i, 0)),
                _whole_weight(w_gate, layer), _whole_weight(w_proj, layer), vec, vec]
    out_shape = ([jax.ShapeDtypeStruct((m, d), F32)] if keep_x else []) + (
        [jax.ShapeDtypeStruct((m, d), next_dtype)] if norm_next else [])
    scratch = [pltpu.VMEM((d, d), BF16), pltpu.VMEM((kp, d), BF16)] + ([] if keep_x else [pltpu.VMEM((tm, d), F32)])
    return pl.pallas_call(
        functools.partial(_ple_body, norm_next=norm_next, keep_x=keep_x),
        out_shape=tuple(out_shape),
        grid=(m // tm,),
        in_specs=in_specs[:len(operands)],
        out_specs=(row,) * len(out_shape),
        scratch_shapes=scratch,
        compiler_params=_params(("arbitrary",)),
        name="ple_update",
    )(*operands)


def _attn_body(slopes_ref, *refs):
    o_ref = refs[9]
    out_ref, lse_ref = refs[10:]
    for hh in range(o_ref.shape[0]):
        _attn_head(slopes_ref, pl.program_id(1) * o_ref.shape[0] + hh, [r.at[hh] for r in refs[:9]],
                   o_ref.at[hh], out_ref.at[hh], lse_ref.at[hh])


def _attn_head(slopes_ref, head, qkv_refs, o_ref, out_ref, lse_ref):
    blk = ATTN_BLOCK

    qi = lax.broadcasted_iota(jnp.int32, (blk, 2 * blk), 0)
    kj = lax.broadcasted_iota(jnp.int32, (blk, 2 * blk), 1)
    dist = qi + blk - kj
    distf = dist.astype(F32)

    blocks = []
    for g, (window, dil) in enumerate(ATTN_GROUPS):
        seq = qkv_refs[3 * g].shape[0]
        nb = seq // dil // blk
        n_back = window // dil
        coef = slopes_ref[g, head] * (float(dil) * LOG2_E)
        bias2 = jnp.where((dist >= 0) & (dist <= n_back), -(coef * distf), NEG_INF)
        bias1 = bias2[:, blk:]
        for j in range(seq // blk):
            blocks.append((g, dil, j // nb, j % nb, j, bias1, bias2))

    def combine(first_row, n_rows):
        rows_per_step = 256
        for c in range(n_rows // rows_per_step):
            sl = pl.ds(first_row + c * rows_per_step, rows_per_step)
            lses = [lse_ref[g, sl, :] for g in range(3)]
            top = jnp.maximum(jnp.maximum(lses[0], lses[1]), lses[2])
            es = [jnp.exp2(x - top) for x in lses]
            num = es[0] * out_ref[0, sl, :] + es[1] * out_ref[1, sl, :] + es[2] * out_ref[2, sl, :]
            o_ref[sl, :] = (num / (es[0] + es[1] + es[2])).astype(o_ref.dtype)

    stage_rows = blk * ATTN_GROUPS[1][1]
    n_stages = o_ref.shape[0] // stage_rows
    wide = [b for b in blocks if blk * b[1] > stage_rows]
    batches = [(wide[b0:b0 + ATTN_BLOCKS_IN_FLIGHT], None) for b0 in range(0, len(wide), ATTN_BLOCKS_IN_FLIGHT)]
    for st in range(n_stages):
        batches.append(([b for b in blocks if blk * b[1] <= stage_rows and (b[3] * blk * b[1]) // stage_rows == st],
                        st))

    for batch, stage in batches:
        s_list, v_list = [], []
        for g, dil, r, n, j, bias1, bias2 in batch:
            q_ref, k_ref, v_ref = qkv_refs[3 * g:3 * g + 3]
            lo = j * blk if n == 0 else (j - 1) * blk
            s = lax.dot_general(q_ref[j * blk:(j + 1) * blk, :], k_ref[lo:(j + 1) * blk, :],
                                (((1,), (1,)), ((), ())), preferred_element_type=F32)
            s_list.append(s + (bias1 if n == 0 else bias2))
            v_list.append(v_ref[lo:(j + 1) * blk, :])
        m_list = [jnp.max(s, axis=1, keepdims=True) for s in s_list]
        p_list = [jnp.exp2(s - m) for s, m in zip(s_list, m_list)]
        l_list = [jnp.sum(p, axis=1, keepdims=True) for p in p_list]
        acc_list = [jnp.dot(p.astype(BF16), v, preferred_element_type=F32) for p, v in zip(p_list, v_list)]
        for (g, dil, r, n, j, _, _), m, l, acc in zip(batch, m_list, l_list, acc_list):
            start = n * blk * dil + r
            rows = pl.ds(start, blk) if dil == 1 else pl.ds(start, blk, stride=dil)
            out_ref[g, rows, :] = acc * (1.0 / l)
            lse_ref[g, rows, :] = jnp.broadcast_to(m + jnp.log2(l), acc.shape)
        if stage is not None:
            combine(stage * stage_rows, stage_rows)


def dilated_attention(qkv, slopes, batch, seq):
    n_g = len(qkv)
    m = qkv[0].shape[1]
    hd = ATTN_HEAD_DIM
    hps = ATTN_HEADS_PER_STEP
    in_specs = [pl.BlockSpec(memory_space=pltpu.SMEM)]
    for g in range(n_g):
        for c in range(3):
            in_specs.append(pl.BlockSpec((hps, seq, hd), lambda b, h, c=c: (c * (ATTN_HEADS // hps) + h, b, 0)))
    return pl.pallas_call(
        _attn_body,
        out_shape=jax.ShapeDtypeStruct((ATTN_HEADS, m, hd), BF16),
        grid=(batch, ATTN_HEADS // hps),
        in_specs=in_specs,
        out_specs=pl.BlockSpec((hps, seq, hd), lambda b, h: (h, b, 0)),
        scratch_shapes=[pltpu.VMEM((hps, n_g, seq, hd), F32)] * 2,
        compiler_params=_params(("arbitrary", "arbitrary")),
        name="dilated_attention",
    )(slopes, *[a for a in qkv for _ in range(3)])


def _split3(x):
    hi = x.astype(BF16)
    r1 = x - hi.astype(F32)
    mid = r1.astype(BF16)
    lo = (r1 - mid.astype(F32)).astype(BF16)
    return hi, mid, lo


def _dot_split2_rhs(x, rhs_bf16):
    hi = x.astype(BF16)
    lo = (x - hi.astype(F32)).astype(BF16)
    return jnp.dot(hi, rhs_bf16, preferred_element_type=F32) + jnp.dot(lo, rhs_bf16, preferred_element_type=F32)


def _dot_exact_lhs(lhs_bf16, x):
    return sum(jnp.dot(lhs_bf16, part, preferred_element_type=F32) for part in _split3(x))


def _dot_t(a, b):
    return lax.dot_general(a, b, (((1,), (1,)), ((), ())), preferred_element_type=F32)


def _t_dot(a, b):
    return lax.dot_general(a, b, (((0,), (0,)), ((), ())), preferred_element_type=F32)


def _scan_chunk_body(r_ref, k_ref, v_ref, a_ref, lw_ref, g_ref, kk_ref, ka_ref, rk_ref, lng_ref, lnb_ref,
                     o_ref, state_ref, lhs_ref, rhs_ref, vs_ref, tail_ref):
    t = SCAN_CHUNK
    n_sub = r_ref.shape[0] // t
    n_pairs = r_ref.shape[1] // LANES
    pairs = range(n_pairs)

    @pl.when(pl.program_id(1) == 0)
    def _():
        state_ref[...] = jnp.zeros_like(state_ref)

    lane = lax.broadcasted_iota(jnp.int32, (t, LANES), 1)
    head0 = lane < RWKV_HEAD_SIZE
    ri = lax.broadcasted_iota(jnp.int32, (2 * t, 2 * t), 0)
    ci = lax.broadcasted_iota(jnp.int32, (2 * t, 2 * t), 1)
    same_head = (ri < t) == (ci < t)
    strict = same_head & (ci < ri)
    incl = same_head & (ci <= ri)
    eye = jnp.where(ri == ci, 1.0, 0.0)
    tri = jnp.where(lax.broadcasted_iota(jnp.int32, (t, t), 1) <= lax.broadcasted_iota(jnp.int32, (t, t), 0),
                    1.0, 0.0).astype(BF16)
    li = lax.broadcasted_iota(jnp.int32, (LANES, LANES), 0)
    lj = lax.broadcasted_iota(jnp.int32, (LANES, LANES), 1)
    head_ones = jnp.where((li < RWKV_HEAD_SIZE) == (lj < RWKV_HEAD_SIZE), 1.0, 0.0).astype(BF16)
    zeros = jnp.zeros_like(head_ones)
    head_ones2 = jnp.concatenate([jnp.concatenate([head_ones, zeros], axis=1),
                                  jnp.concatenate([zeros, head_ones], axis=1)], axis=0)
    n_doublings = (t - 1).bit_length()

    def lanes(x, p):
        return x[:, p * LANES:(p + 1) * LANES]

    def stack(x):
        return jnp.concatenate([jnp.where(head0, x, 0.0), jnp.where(head0, 0.0, x)], axis=0)

    def tall(x):
        return jnp.concatenate([lanes(x, p) for p in pairs], axis=0)

    def wide(s):
        return jnp.concatenate([s[p * t:(p + 1) * t, :] for p in pairs], axis=1)

    def head_sum(x):
        return wide(_dot_split2_rhs(tall(x), head_ones))

    def head_sum_two(x1, x2):
        s = _dot_split2_rhs(jnp.concatenate([tall(x1), tall(x2)], axis=1), head_ones2)
        return wide(s[:, :LANES]), wide(s[:, LANES:])

    def mm(a, b):
        return jnp.dot(a, b, preferred_element_type=F32)

    inv_n = 1.0 / RWKV_HEAD_SIZE

    def operands(c):
        rows = pl.ds(c * t, t)
        r = r_ref[rows, :]
        k0 = k_ref[rows, :]
        a_gate = a_ref[rows, :]
        lw = lw_ref[rows, :]
        kk = k0 * kk_ref[...]
        k = k0 * (1.0 + (a_gate - 1.0) * ka_ref[...])
        kk_sq, rk_sum = head_sum_two(kk * kk, r * k * rk_ref[...])
        kk = kk / jnp.maximum(jnp.sqrt(kk_sq), 1e-12)
        b = kk * a_gate
        lg = _dot_exact_lhs(tri, lw)
        g_incl = jnp.exp(lg)
        g_inv = jnp.exp(-lg)
        g_last = g_incl[t - 1:t, :]
        g_tail = g_last * g_inv
        at = -kk * jnp.exp(lg - lw)
        rt = r * g_incl
        bt = b * g_inv
        kt = k * g_inv
        bh = b * g_tail
        kh = k * g_tail
        for p in pairs:
            lhs_ref[c, p, :2 * t, :] = stack(lanes(at, p)).astype(BF16)
            lhs_ref[c, p, 2 * t:, :] = stack(lanes(rt, p)).astype(BF16)
            btp = lanes(bt, p).astype(BF16)
            ktp = lanes(kt, p).astype(BF16)
            rhs_ref[c, p] = jnp.concatenate([btp, btp, ktp, ktp], axis=0)
            vs_ref[c, p] = stack(lanes(v_ref[rows, :], p)).astype(BF16)
            tail_ref[c, p, :2 * t, :] = stack(lanes(bh, p)).astype(BF16)
            tail_ref[c, p, 2 * t:, :] = stack(lanes(kh, p)).astype(BF16)
        return g_last, rk_sum

    def triangular(c):
        n_ab, n_ak, b_r = [], [], []
        for p in pairs:
            sc = _dot_t(lhs_ref[c, p], rhs_ref[c, p])
            n_ab.append(jnp.where(strict, sc[:2 * t, :2 * t], 0.0))
            n_ak.append(jnp.where(strict, sc[:2 * t, 2 * t:], 0.0).astype(BF16))
            b_r.append(jnp.concatenate([jnp.where(incl, sc[2 * t:, :2 * t], 0.0),
                                        jnp.where(incl, sc[2 * t:, 2 * t:], 0.0)], axis=1).astype(BF16))
        inv = [eye + n for n in n_ab]
        pw = n_ab
        for _ in range(n_doublings - 1):
            pwb = [x.astype(BF16) for x in pw]
            pw = [mm(x, x) for x in pwb]
            inv = [i_p + mm(q.astype(BF16), i_p.astype(BF16)) for i_p, q in zip(inv, pw)]
        return n_ak, b_r, [x.astype(BF16) for x in inv]

    def advance(c, g_last, n_ak, b_r, inv):
        state = [state_ref[p] for p in pairs]
        state_b = [s.astype(BF16) for s in state]
        w = [_dot_t(lhs_ref[c, p, :2 * t, :], state_b[p]) + mm(n_ak[p], vs_ref[c, p]) for p in pairs]
        zv = [jnp.concatenate([mm(inv[p], w[p].astype(BF16)).astype(BF16), vs_ref[c, p]], axis=0) for p in pairs]
        y_parts = []
        for p in pairs:
            y_s = _dot_t(lhs_ref[c, p, 2 * t:, :], state_b[p]) + mm(b_r[p], zv[p])
            y_parts.append(y_s[:t, :] + y_s[t:, :])
            state_ref[p] = state[p] * lanes(g_last, p) + _t_dot(zv[p], tail_ref[c, p])
        return jnp.concatenate(y_parts, axis=1)

    def finish(c, y, y_sum, rk_sum):
        rows = pl.ds(c * t, t)
        yc = y - y_sum * inv_n
        var = head_sum(yc * yc) * inv_n
        yn = yc * lax.rsqrt(var + RWKV_GN_EPS) * lng_ref[...] + lnb_ref[...]
        o_ref[rows, :] = ((yn + rk_sum * v_ref[rows, :]) * g_ref[rows, :]).astype(o_ref.dtype)

    prepared = [operands(c) for c in range(n_sub)]
    solved = [triangular(c) for c in range(n_sub)]
    pending = None
    for c in range(n_sub):
        y = advance(c, prepared[c][0], *solved[c])
        if pending is not None:
            finish(*pending)
        pending = (c, y, head_sum(y), prepared[c][1])
    finish(*pending)


def rwkv_scan(rkv, a_gate, log_w, gate, k_k, k_a, r_k, ln_g, ln_b, batch, seq):
    _, m, d = rkv.shape
    t = SCAN_CHUNK
    n_sub = SCAN_CHUNKS_PER_STEP
    nc = seq // (n_sub * t)
    n_pairs = d // LANES
    row = lambda g: pl.BlockSpec((None, n_sub * t, d), lambda b, c, g=g: (g, b * nc + c, 0))
    tile = pl.BlockSpec((n_sub * t, d), lambda b, c: (b * nc + c, 0))
    vec = pl.BlockSpec((1, d), lambda b, c: (0, 0))
    return pl.pallas_call(
        _scan_chunk_body,
        out_shape=jax.ShapeDtypeStruct((m, d), BF16),
        grid=(batch, nc),
        in_specs=[row(0), row(1), row(2), tile, tile, tile, vec, vec, vec, vec, vec],
        out_specs=tile,
        scratch_shapes=[pltpu.VMEM((n_pairs, LANES, LANES), F32),
                        pltpu.VMEM((n_sub, n_pairs, 4 * t, LANES), BF16),
                        pltpu.VMEM((n_sub, n_pairs, 4 * t, LANES), BF16),
                        pltpu.VMEM((n_sub, n_pairs, 2 * t, LANES), BF16),
                        pltpu.VMEM((n_sub, n_pairs, 4 * t, LANES), BF16)],
        compiler_params=_params(("arbitrary", "arbitrary")),
        name="rwkv_scan",
    )(rkv, rkv, rkv, a_gate, log_w, gate,
      k_k.reshape(1, d), k_a.reshape(1, d), r_k.reshape(1, d), ln_g.reshape(1, d), ln_b.reshape(1, d))


def _alibi_slopes(n_groups, n_heads):
    n = n_groups * n_heads
    idx = jnp.arange(1, n + 1, dtype=F32)
    return (2.0 ** (-8.0 * idx / n)).reshape(n_groups, n_heads)


def kernel(x, p, attn_norm, attn_w_qkv, attn_w_o, rwkv_norm, rwkv_mu, rwkv_w_rkv, rwkv_w0, rwkv_w_w1, rwkv_w_w2, rwkv_a0, rwkv_w_a1, rwkv_w_a2, rwkv_w_g1, rwkv_w_g2, rwkv_k_k, rwkv_k_a, rwkv_r_k, rwkv_ln_g, rwkv_ln_b, rwkv_w_o, conv_norm, conv_w_in, conv_w, conv_w_out, ffn_norm, ffn_w_gu, ffn_conv_w, ffn_conv_b, ffn_w_down, ple_w_proj, ple_norm, ple_w_gate, final_norm):
    batch, seq, d = x.shape
    depth = p.shape[0]
    m = batch * seq
    n_groups = len(ATTN_GROUPS)
    slopes = _alibi_slopes(n_groups, ATTN_HEADS)
    x = x.reshape(m, d)
    p = p.reshape(depth, m, p.shape[-1])

    h_next = None
    for i in range(depth):
        kind, j = i % 3, i // 3
        if kind == 0:
            hs = attn_rmsnorm(x, attn_norm[j], batch, seq)
            n_q = ATTN_HEADS * ATTN_HEAD_DIM
            q_scale = jnp.concatenate([jnp.full((1, n_q), ATTN_HEAD_DIM ** -0.5 * LOG2_E, F32),
                                       jnp.ones((1, 2 * n_q), F32)], axis=1)
            qkv = [linear(hs[g], attn_w_qkv, j, BF16, tm=2048, n_out=3 * n_q, first_col=g * 3 * n_q,
                          col_scale=q_scale, slab_major=True, name="attn_qkv") for g in range(n_groups)]
            y = dilated_attention(qkv, slopes, batch, seq)
            w_out = attn_w_o
        elif kind == 1:
            xs, log_w, a_gate, gate = rwkv_prep(
                x, rwkv_norm[j], rwkv_mu[j], rwkv_w0[j], rwkv_a0[j],
                [rwkv_w_w1[j], rwkv_w_w2[j], rwkv_w_a1[j], rwkv_w_a2[j], rwkv_w_g1[j], rwkv_w_g2[j]], seq)
            rkv = grouped_linear(xs, rwkv_w_rkv, j, F32, tm=2048, tn=1024, name="rwkv_rkv")
            y = rwkv_scan(rkv, a_gate, log_w, gate, rwkv_k_k[j], rwkv_k_a[j], rwkv_r_k[j].reshape(d),
                          rwkv_ln_g[j], rwkv_ln_b[j], batch, seq)
            w_out = rwkv_w_o
        else:
            y = shortconv_in(h_next, conv_w_in, j, conv_w[j], seq, tm=2048, tn=256)
            w_out = conv_w_out
        x, h = project_residual_norm(y, w_out, j, x, ffn_norm[i])
        act, w_down = ffn_gate_up(h, ffn_w_gu, i, ffn_conv_w[i], ffn_conv_b[i], seq, ffn_w_down, tm=2048)
        x = linear(act, w_down, None, F32, tm=512, tn=1024, residual=x, name="ffn_down")
        if i + 1 == depth:
            out, = ple_update(x, p, i, ple_w_gate, ple_w_proj, ple_norm[i], final_norm, F32, keep_x=False)
            return out.reshape(batch, seq, d)
        if (i + 1) % 3 == 2:
            x, h_next = ple_update(x, p, i, ple_w_gate, ple_w_proj, ple_norm[i], conv_norm[(i + 1) // 3])
        else:
            x, = ple_update(x, p, i, ple_w_gate, ple_w_proj, ple_norm[i])
```
